```python
import math
import jax, jax.numpy as jnp
from jax import lax
import numpy as np

D_MODEL = 1024
BATCH = 16
SEQ = 4096
DEPTH = 1

MEM_LEN = 256
NORM_EPS = 1e-6
SSD_HEAD_DIM = 64
SSD_HEADS = D_MODEL // SSD_HEAD_DIM
SSD_INNER = SSD_HEADS * SSD_HEAD_DIM
SSD_GROUPS = 4
SSD_HEADS_PER_GROUP = SSD_HEADS // SSD_GROUPS
SSD_STATE = 128
SSD_CONV = 4
SSD_CHUNK = 128
SSD_CONV_CH = SSD_INNER + 2 * SSD_GROUPS * SSD_STATE
FOX_HEAD_DIM = 128
FOX_HEADS = D_MODEL // FOX_HEAD_DIM
FOX_INNER = FOX_HEADS * FOX_HEAD_DIM
Q_BLOCK = 128
N_BRANCH = 2
IN_SPLITS = (SSD_INNER, SSD_CONV_CH, SSD_HEADS, FOX_INNER, FOX_INNER, FOX_INNER, FOX_HEADS, N_BRANCH * D_MODEL)
IN_COLS = sum(IN_SPLITS)
X_HEADS = 4
X_HEAD_DIM = D_MODEL // X_HEADS
N_EXPERTS = 32
TOP_K = 4
D_EXPERT = D_MODEL
SWIGLU_ALPHA = 1.702
SWIGLU_LIMIT = 7.0
EXPERT_BLOCK = 128

kernel_name = 'hybrid_ssd_fox_moe_layer'


def rms_norm(x, g):
    xf = x.astype(jnp.float32)
    y = xf * lax.rsqrt(jnp.mean(xf * xf, axis=-1, keepdims=True) + NORM_EPS)
    return (y * g.astype(jnp.float32)).astype(x.dtype)


def causal_depthwise_conv(x, w, b):
    ch = x.shape[-1]
    y = lax.conv_general_dilated(x, w[:, None, :].astype(x.dtype), window_strides=(1,),
                                 padding=[(SSD_CONV - 1, 0)],
                                 dimension_numbers=('NWC', 'WIO', 'NWC'),
                                 feature_group_count=ch)
    return y + b


def ssd_branch(z, xbc, dt_raw, conv_w, conv_b, dt_bias, a_log, d_skip, norm_g):
    Bsz, S, _ = z.shape
    G, E, P, N, Lc = SSD_GROUPS, SSD_HEADS_PER_GROUP, SSD_HEAD_DIM, SSD_STATE, SSD_CHUNK
    nc = S // Lc
    xbc = jax.nn.silu(causal_depthwise_conv(xbc, conv_w, conv_b))
    xs, b_ssm, c_ssm = jnp.split(xbc, [SSD_INNER, SSD_INNER + G * N], axis=-1)
    dt = jax.nn.softplus(dt_raw.astype(jnp.float32) + dt_bias)
    a = -jnp.exp(a_log.astype(jnp.float32)) * dt
    X = xs.reshape(Bsz, nc, Lc, G, E, P)
    Xdt = X * dt.reshape(Bsz, nc, Lc, G, E, 1)
    Bc = b_ssm.reshape(Bsz, nc, Lc, G, N)
    Cc = c_ssm.reshape(Bsz, nc, Lc, G, N)
    a = a.reshape(Bsz, nc, Lc, G, E).transpose(0, 1, 3, 4, 2)
    a_cum = jnp.cumsum(a, axis=-1)
    diff = a_cum[..., :, None] - a_cum[..., None, :]
    causal = jnp.tril(jnp.ones((Lc, Lc), dtype=bool))
    decay_mat = jnp.exp(jnp.where(causal, diff, -jnp.inf))
    cb = jnp.einsum('bclgn,bcsgn->bcgls', Cc, Bc)
    y_diag = jnp.einsum('bcgels,bcsgep->bclgep', cb[:, :, :, None] * decay_mat, Xdt)
    decay_states = jnp.exp(a_cum[..., -1:] - a_cum)
    chunk_states = jnp.einsum('bclgn,bcgel,bclgep->bcgepn', Bc, decay_states, Xdt)
    chunk_decay = jnp.exp(a_cum[..., -1])

    def step(state, inp):
        dec, new = inp
        return dec[..., None, None] * state + new, state

    init = jnp.zeros((Bsz, G, E, P, N), jnp.float32)
    _, states_in = lax.scan(step, init, (chunk_decay.transpose(1, 0, 2, 3),
                                         chunk_states.transpose(1, 0, 2, 3, 4, 5).astype(jnp.float32)))
    states_in = states_in.transpose(1, 0, 2, 3, 4, 5)
    y_off = jnp.einsum('bclgn,bcgepn,bcgel->bclgep', Cc, states_in, jnp.exp(a_cum))
    y = y_diag + y_off + d_skip.reshape(G, E, 1) * X
    y = y.reshape(Bsz, S, G, E * P)
    gated = y * jax.nn.silu(z).reshape(Bsz, S, G, E * P)
    out = rms_norm(gated, norm_g.reshape(G, E * P))
    return out.reshape(Bsz, S, SSD_INNER).astype(z.dtype)


def fox_attention(q, k, v, log_f):
    Bsz, S, H, dh = q.shape
    nb = S // Q_BLOCK
    scale = dh ** -0.5
    c = jnp.cumsum(log_f, axis=1)
    c_keys = c.transpose(0, 2, 1)
    qb = q.reshape(Bsz, nb, Q_BLOCK, H, dh).transpose(1, 0, 2, 3, 4)
    cqb = c.reshape(Bsz, nb, Q_BLOCK, H).transpose(1, 0, 2, 3)
    kpos = jnp.arange(S)

    def block(args):
        qi, ci, i = args
        s = jnp.einsum('bqhd,bkhd->bhqk', qi, k).astype(jnp.float32) * scale
        s = s + ci.transpose(0, 2, 1)[..., None] - c_keys[:, :, None, :]
        qpos = i * Q_BLOCK + jnp.arange(Q_BLOCK)
        s = jnp.where(kpos[None, :] <= qpos[:, None], s, -jnp.inf)
        p = jax.nn.softmax(s, axis=-1)
        return jnp.einsum('bhqk,bkhd->bqhd', p.astype(v.dtype), v)

    out = lax.map(block, (qb, cqb, jnp.arange(nb)))
    return out.transpose(1, 0, 2, 3, 4).reshape(Bsz, S, H * dh)


def memory_cross_attention(u, mem_n, w_q, w_kv, w_o):
    Bsz, S, D = u.shape
    M = mem_n.shape[1]
    q = (u @ w_q).reshape(Bsz, S, X_HEADS, X_HEAD_DIM)
    kv = (mem_n @ w_kv).reshape(Bsz, M, 2, X_HEADS, X_HEAD_DIM)
    k, v = kv[:, :, 0], kv[:, :, 1]
    s = jnp.einsum('bshd,bmhd->bhsm', q, k).astype(jnp.float32) * (X_HEAD_DIM ** -0.5)
    p = jax.nn.softmax(s, axis=-1)
    o = jnp.einsum('bhsm,bmhd->bshd', p.astype(v.dtype), v).reshape(Bsz, S, X_HEADS * X_HEAD_DIM)
    return o @ w_o


def moe_ffn(u, w_router, b_router, w1, b1, w2, b2):
    Bsz, S, D = u.shape
    xf = u.reshape(-1, D)
    n_tok = xf.shape[0]
    logits = (xf @ w_router + b_router).astype(jnp.float32)
    top_vals, top_idx = lax.top_k(logits, TOP_K)
    gates = jax.nn.softmax(top_vals, axis=-1)
    nk = n_tok * TOP_K
    flat_e = top_idx.reshape(-1)
    flat_tok = jnp.arange(nk, dtype=jnp.int32) // TOP_K
    flat_w = gates.reshape(-1)
    order = jnp.argsort(flat_e)
    se = flat_e[order]
    counts = jnp.zeros((N_EXPERTS,), jnp.int32).at[flat_e].add(1)
    padded = (counts + EXPERT_BLOCK - 1) // EXPERT_BLOCK * EXPERT_BLOCK
    start = jnp.cumsum(counts) - counts
    pend = jnp.cumsum(padded)
    pstart = pend - padded
    dest = pstart[se] + (jnp.arange(nk, dtype=jnp.int32) - start[se])
    n_blocks = (nk + EXPERT_BLOCK - 1) // EXPERT_BLOCK + N_EXPERTS
    n_rows = n_blocks * EXPERT_BLOCK
    row_tok = jnp.full((n_rows,), n_tok, jnp.int32).at[dest].set(flat_tok[order])
    row_w = jnp.zeros((n_rows,), jnp.float32).at[dest].set(flat_w[order])
    blk_e = jnp.minimum(jnp.searchsorted(pend, jnp.arange(n_blocks, dtype=jnp.int32) * EXPERT_BLOCK,
                                         side='right'), N_EXPERTS - 1)
    x_pad = jnp.concatenate([xf, jnp.zeros((1, D), xf.dtype)], axis=0)
    x_rows = x_pad[row_tok].reshape(n_blocks, EXPERT_BLOCK, D)

    def expert_block(args):
        xb, e = args
        hgl = xb @ w1[e] + b1[e]
        glu = jnp.minimum(hgl[:, :D_EXPERT], SWIGLU_LIMIT)
        lin = jnp.clip(hgl[:, D_EXPERT:], -SWIGLU_LIMIT, SWIGLU_LIMIT)
        act = glu * jax.nn.sigmoid(SWIGLU_ALPHA * glu) * (lin + 1.0)
        return act @ w2[e] + b2[e]

    y_rows = lax.map(expert_block, (x_rows, blk_e)).reshape(n_rows, D)
    y_rows = y_rows * row_w[:, None].astype(y_rows.dtype)
    out = jax.ops.segment_sum(y_rows, row_tok, num_segments=n_tok + 1)[:n_tok]
    return out.reshape(Bsz, S, D).astype(u.dtype)


def hybrid_layer(h, mem, norm_mix_g, w_in, conv_w, conv_b, dt_bias, a_log, d_skip, ssd_norm_g,
                 w_ssd_branch, b_forget, w_fox_branch, b_gates, w_out, norm_cross_g, norm_mem_g,
                 w_cross_q, w_cross_kv, w_cross_o, norm_ffn_g, w_router, b_router,
                 w_expert_in, b_expert_in, w_expert_out, b_expert_out):
    Bsz, S, D = h.shape
    u = rms_norm(h, norm_mix_g)
    proj = u @ w_in
    cuts = np.cumsum(IN_SPLITS)[:-1].tolist()
    z, xbc, dt_raw, q, k, v, f_logit, gate_logit = jnp.split(proj, cuts, axis=-1)
    y_ssd = ssd_branch(z, xbc, dt_raw, conv_w, conv_b, dt_bias, a_log, d_skip, ssd_norm_g)
    log_f = jax.nn.log_sigmoid((f_logit + b_forget).astype(jnp.float32))
    hs = (Bsz, S, FOX_HEADS, FOX_HEAD_DIM)
    y_fox = fox_attention(q.reshape(hs), k.reshape(hs), v.reshape(hs), log_f)
    gates = jax.nn.sigmoid(gate_logit + b_gates.reshape(-1)).reshape(Bsz, S, N_BRANCH, D)
    merged = gates[:, :, 0] * (y_ssd @ w_ssd_branch) + gates[:, :, 1] * (y_fox @ w_fox_branch)
    h = h + merged @ w_out
    h = h + memory_cross_attention(rms_norm(h, norm_cross_g), rms_norm(mem, norm_mem_g),
                                   w_cross_q, w_cross_kv, w_cross_o)
    h = h + moe_ffn(rms_norm(h, norm_ffn_g), w_router, b_router,
                    w_expert_in, b_expert_in, w_expert_out, b_expert_out)
    return h


def setup_inputs(seed: int = 0) -> dict:
    key = jax.random.key(seed)
    ks = jax.random.split(key, 32)
    f32 = jnp.float32
    L = DEPTH

    def normal(k, shape, scale):
        return jax.random.normal(k, shape, f32) * scale

    def gain(k, shape):
        return 1.0 + 0.05 * jax.random.normal(k, shape, f32)

    dt0 = jnp.exp(jax.random.uniform(ks[6], (L, SSD_HEADS), f32, math.log(1e-3), math.log(1e-1)))
    return {
        'x': normal(ks[0], (BATCH, SEQ, D_MODEL), 1.0),
        'mem': normal(ks[1], (BATCH, MEM_LEN, D_MODEL), 1.0),
        'norm_mix_g': gain(ks[2], (L, D_MODEL)),
        'w_in': normal(ks[3], (L, D_MODEL, IN_COLS), D_MODEL ** -0.5),
        'conv_w': normal(ks[4], (L, SSD_CONV, SSD_CONV_CH), SSD_CONV ** -0.5),
        'conv_b': normal(ks[5], (L, SSD_CONV_CH), 0.02),
        'dt_bias': dt0 + jnp.log(-jnp.expm1(-dt0)),
        'a_log': jnp.log(jax.random.uniform(ks[7], (L, SSD_HEADS), f32, 1.0, 16.0)),
        'd_skip': gain(ks[8], (L, SSD_HEADS)),
        'ssd_norm_g': gain(ks[9], (L, SSD_INNER)),
        'w_ssd_branch': normal(ks[10], (L, SSD_INNER, D_MODEL), SSD_INNER ** -0.5),
        'b_forget': 2.0 + 0.5 * jax.random.normal(ks[11], (L, FOX_HEADS), f32),
        'w_fox_branch': normal(ks[12], (L, FOX_INNER, D_MODEL), FOX_INNER ** -0.5),
        'b_gates': normal(ks[13], (L, N_BRANCH, D_MODEL), 0.02),
        'w_out': normal(ks[14], (L, D_MODEL, D_MODEL), D_MODEL ** -0.5),
        'norm_cross_g': gain(ks[15], (L, D_MODEL)),
        'norm_mem_g': gain(ks[16], (L, D_MODEL)),
        'w_cross_q': normal(ks[17], (L, D_MODEL, X_HEADS * X_HEAD_DIM), D_MODEL ** -0.5),
        'w_cross_kv': normal(ks[18], (L, D_MODEL, 2 * X_HEADS * X_HEAD_DIM), D_MODEL ** -0.5),
        'w_cross_o': normal(ks[19], (L, X_HEADS * X_HEAD_DIM, D_MODEL), D_MODEL ** -0.5),
        'norm_ffn_g': gain(ks[20], (L, D_MODEL)),
        'w_router': normal(ks[21], (L, D_MODEL, N_EXPERTS), D_MODEL ** -0.5),
        'b_router': normal(ks[22], (L, N_EXPERTS), 0.01),
        'w_expert_in': normal(ks[23], (L, N_EXPERTS, D_MODEL, 2 * D_EXPERT), D_MODEL ** -0.5),
        'b_expert_in': normal(ks[24], (L, N_EXPERTS, 2 * D_EXPERT), 0.02),
        'w_expert_out': normal(ks[25], (L, N_EXPERTS, D_EXPERT, D_MODEL), D_EXPERT ** -0.5),
        'b_expert_out': normal(ks[26], (L, N_EXPERTS, D_MODEL), 0.02),
        'norm_final_g': gain(ks[27], (D_MODEL,)),
    }


def reference(x, mem, norm_mix_g, w_in, conv_w, conv_b, dt_bias, a_log, d_skip, ssd_norm_g,
              w_ssd_branch, b_forget, w_fox_branch, b_gates, w_out, norm_cross_g, norm_mem_g,
              w_cross_q, w_cross_kv, w_cross_o, norm_ffn_g, w_router, b_router,
              w_expert_in, b_expert_in, w_expert_out, b_expert_out, norm_final_g):
    h = x
    for l in range(DEPTH):
        h = hybrid_layer(h, mem, norm_mix_g[l], w_in[l], conv_w[l], conv_b[l], dt_bias[l], a_log[l],
                         d_skip[l], ssd_norm_g[l], w_ssd_branch[l], b_forget[l], w_fox_branch[l],
                         b_gates[l], w_out[l], norm_cross_g[l], norm_mem_g[l], w_cross_q[l],
                         w_cross_kv[l], w_cross_o[l], norm_ffn_g[l], w_router[l], b_router[l],
                         w_expert_in[l], b_expert_in[l], w_expert_out[l], b_expert_out[l])
    return rms_norm(h, norm_final_g)
```

```python
import functools

import jax
import jax.numpy as jnp
from jax import lax
from jax.experimental import pallas as pl
from jax.experimental.pallas import tpu as pltpu

F32 = jnp.float32
BF16 = jnp.bfloat16

NORM_EPS = 1e-6
SSD_HEAD_DIM = 64
SSD_HEADS = 16
SSD_GROUPS = 4
SSD_STATE = 128
SSD_CONV = 4
SSD_CHUNK = 128
FOX_HEADS = 8
FOX_HEAD_DIM = 128
X_HEADS = 4
N_EXPERTS = 32
TOP_K = 4
SWIGLU_ALPHA = 1.702
SWIGLU_LIMIT = 7.0

LANES = 128
CONV_HALO = 8
EXPERT_ROWS = 512
NEG_BIG = -1e30
VMEM_LIMIT = 56 * 1024 * 1024

COL_XBC, COL_Z, COL_Q, COL_K, COL_V, COL_GATE, MAIN_COLS = 0, 2048, 3072, 4096, 5120, 6144, 8192
LANE_DT, LANE_F = 0, 16
LANE_IDX, LANE_POS, LANE_GATE = 0, 4, 8


def _sigmoid(x):
    return 1.0 / (1.0 + jnp.exp(-x))


def _rms(x, g):
    ms = jnp.mean(x * x, axis=-1, keepdims=True)
    return x * lax.rsqrt(ms + NORM_EPS) * g


def _dot(a, b):
    return jnp.dot(a, b, preferred_element_type=F32)


def _dot_nt(a, b):
    return lax.dot_general(a, b, (((1,), (1,)), ((), ())), preferred_element_type=F32)


def _split_dot(v, m_bf16, terms):
    out = None
    for _ in range(terms):
        hi = v.astype(BF16)
        part = _dot(hi, m_bf16)
        out = part if out is None else out + part
        v = v - hi.astype(F32)
    return out


def _params(sem, vmem=VMEM_LIMIT):
    return pltpu.CompilerParams(dimension_semantics=sem, vmem_limit_bytes=vmem)


def _inproj_kernel(x_ref, g_ref, wm_ref, ws_ref, main_ref, small_ref, u_ref):
    @pl.when(pl.program_id(1) == 0)
    def _():
        u = _rms(x_ref[...], g_ref[...]).astype(BF16)
        u_ref[...] = u
        small_ref[...] = _dot(u, ws_ref[...])

    main_ref[...] = _dot(u_ref[...], wm_ref[...]).astype(BF16)


def _inproj(x2, g, w_main, w_small, tm, tn):
    n, d = x2.shape
    return pl.pallas_call(
        _inproj_kernel,
        grid=(n // tm, MAIN_COLS // tn),
        in_specs=[
            pl.BlockSpec((tm, d), lambda i, j: (i, 0)),
            pl.BlockSpec((1, d), lambda i, j: (0, 0)),
            pl.BlockSpec((d, tn), lambda i, j: (0, j)),
            pl.BlockSpec((d, LANES), lambda i, j: (0, 0)),
        ],
        out_specs=[
            pl.BlockSpec((tm, tn), lambda i, j: (i, j)),
            pl.BlockSpec((tm, LANES), lambda i, j: (i, 0)),
        ],
        out_shape=[jax.ShapeDtypeStruct((n, MAIN_COLS), BF16), jax.ShapeDtypeStruct((n, LANES), F32)],
        scratch_shapes=[pltpu.VMEM((tm, d), BF16)],
        compiler_params=_params(("arbitrary", "arbitrary")),
        name="inproj",
    )(x2, g, w_main, w_small)


def _ssd_kernel(xbc_ref, z_ref, sm_ref, cw_ref, cb_ref, bias_ref, alog_ref, dsk_ref, ng_ref, e_ref,
                y_ref, c_ref, ct_ref, state, buf, ccarry):
    L = SSD_CHUNK
    inner = SSD_HEADS * SSD_HEAD_DIM
    gw = inner // SSD_GROUPS
    hpg = SSD_HEADS // SSD_GROUPS

    @pl.when(pl.program_id(1) == 0)
    def _():
        state[...] = jnp.zeros_like(state)
        buf[0:CONV_HALO, :] = jnp.zeros((CONV_HALO, buf.shape[1]), F32)
        ccarry[...] = jnp.zeros_like(ccarry)

    xbc = xbc_ref[0].astype(F32)
    buf[CONV_HALO:CONV_HALO + L, :] = xbc
    acc = cb_ref[...] + cw_ref[SSD_CONV - 1:SSD_CONV, :] * xbc
    for j in range(SSD_CONV - 1):
        acc = acc + cw_ref[j:j + 1, :] * buf[pl.ds(CONV_HALO - (SSD_CONV - 1) + j, L), :]
    buf[0:CONV_HALO, :] = xbc[L - CONV_HALO:L, :]
    xc = acc * _sigmoid(acc)
    xs = xc[:, :inner]
    bm = xc[:, inner:inner + SSD_GROUPS * SSD_STATE]
    cm = xc[:, inner + SSD_GROUPS * SSD_STATE:]

    sm = sm_ref[0] + bias_ref[...]
    t = jnp.log1p(jnp.exp(-jnp.abs(sm)))
    dt = jnp.maximum(sm, 0.0) + t
    logf = jnp.minimum(sm, 0.0) - t
    lane = lax.broadcasted_iota(jnp.int32, (L, LANES), 1)
    is_dt = lane < LANE_F
    a = -jnp.exp(alog_ref[...]) * dt
    comb = jnp.where(is_dt, a, jnp.where(lane < LANE_F + FOX_HEADS, logf, 0.0))
    row = lax.broadcasted_iota(jnp.int32, (L, L), 0)
    col = lax.broadcasted_iota(jnp.int32, (L, L), 1)
    causal = row >= col
    tril = jnp.where(causal, 1.0, 0.0).astype(BF16)
    cum = _split_dot_left(tril, comb)
    cglob = cum + ccarry[...]
    ccarry[...] = cglob[L - 1:L, :]
    c_ref[0] = cglob
    ct_ref[0] = cglob.T

    acum = jnp.where(is_dt, cum, 0.0)
    act = acum.T
    alast = acum[L - 1:L, :]
    e_mat = e_ref[...]
    dt_e = _split_dot(jnp.where(is_dt, dt, 0.0), e_mat, 2)
    ds_e = _split_dot(jnp.where(is_dt, jnp.exp(alast - acum), 0.0), e_mat, 2)
    ea_e = _split_dot(jnp.where(is_dt, jnp.exp(acum), 0.0), e_mat, 2)
    cd_e = _split_dot(jnp.broadcast_to(jnp.where(is_dt[0:1], jnp.exp(alast), 0.0), (8, LANES)), e_mat, 2)[0:1]

    xdt = xs * dt_e
    xdt_b = xdt.astype(BF16)
    xds_b = (xdt * ds_e).astype(BF16)
    glane = lax.broadcasted_iota(jnp.int32, (L, gw), 1)
    ys = []
    for g in range(SSD_GROUPS):
        gs = slice(g * gw, (g + 1) * gw)
        bg = bm[:, g * SSD_STATE:(g + 1) * SSD_STATE]
        cg = cm[:, g * SSD_STATE:(g + 1) * SSD_STATE].astype(BF16)
        cb = _dot_nt(cg, bg.astype(BF16))
        xg = xdt_b[:, gs]
        ydiag = jnp.zeros((L, gw), F32)
        for e in range(hpg):
            h = g * hpg + e
            diff = acum[:, h:h + 1] - act[h:h + 1, :]
            dm = jnp.exp(jnp.where(causal, diff, -jnp.inf))
            yfull = _dot((cb * dm).astype(BF16), xg)
            ydiag = jnp.where((glane >= e * SSD_HEAD_DIM) & (glane < (e + 1) * SSD_HEAD_DIM), yfull, ydiag)
        stg = state[:, gs]
        yoff = _dot(cg, stg.astype(BF16)) * ea_e[:, gs]
        state[:, gs] = stg * cd_e[:, gs] + _dot(bg.T.astype(BF16), xds_b[:, gs])
        ys.append(ydiag + yoff)
    y = jnp.concatenate(ys, axis=1) + dsk_ref[...] * xs

    z = z_ref[0].astype(F32)
    gated = y * (z * _sigmoid(z))
    outs = []
    for g in range(SSD_GROUPS):
        seg = gated[:, g * gw:(g + 1) * gw]
        ms = jnp.mean(seg * seg, axis=-1, keepdims=True)
        outs.append(seg * lax.rsqrt(ms + NORM_EPS))
    y_ref[0] = (jnp.concatenate(outs, axis=1) * ng_ref[...]).astype(BF16)


def _split_dot_left(m_bf16, v):
    out = None
    for _ in range(3):
        hi = v.astype(BF16)
        part = _dot(m_bf16, hi)
        out = part if out is None else out + part
        v = v - hi.astype(F32)
    return out


def _ssd(main3, small3, conv_w, conv_b, bias_row, alog_row, dskip_row, normg_row, e_mat):
    b, s, _ = main3.shape
    L = SSD_CHUNK
    inner = SSD_HEADS * SSD_HEAD_DIM
    cc = conv_w.shape[1]
    const = lambda shape: pl.BlockSpec(shape, lambda i, c: (0,) * len(shape))
    return pl.pallas_call(
        _ssd_kernel,
        grid=(b, s // L),
        in_specs=[
            pl.BlockSpec((1, L, cc), lambda i, c: (i, c, COL_XBC // cc)),
            pl.BlockSpec((1, L, inner), lambda i, c: (i, c, COL_Z // inner)),
            pl.BlockSpec((1, L, LANES), lambda i, c: (i, c, 0)),
            const((SSD_CONV, cc)), const((1, cc)), const((1, LANES)), const((1, LANES)),
            const((1, inner)), const((1, inner)), const((LANES, inner)),
        ],
        out_specs=[
            pl.BlockSpec((1, L, inner), lambda i, c: (i, c, 0)),
            pl.BlockSpec((1, L, LANES), lambda i, c: (i, c, 0)),
            pl.BlockSpec((1, LANES, L), lambda i, c: (i, 0, c)),
        ],
        out_shape=[
            jax.ShapeDtypeStruct((b, s, inner), BF16),
            jax.ShapeDtypeStruct((b, s, LANES), F32),
            jax.ShapeDtypeStruct((b, LANES, s), F32),
        ],
        scratch_shapes=[
            pltpu.VMEM((SSD_STATE, inner), F32),
            pltpu.VMEM((L + CONV_HALO, cc), F32),
            pltpu.VMEM((1, LANES), F32),
        ],
        compiler_params=_params(("arbitrary", "arbitrary")),
        name="ssd",
    )(main3, main3, small3, conv_w, conv_b, bias_row, alog_row, dskip_row, normg_row, e_mat)


def _fox_kernel(q_ref, k_ref, v_ref, cq_ref, ck_ref, o_ref, m_sc, l_sc, acc_sc, *, tq):
    h = pl.program_id(1)
    i = pl.program_id(2)
    scale = FOX_HEAD_DIM ** -0.5
    q = q_ref[0]
    lane = lax.broadcasted_iota(jnp.int32, (tq, LANES), 1)
    cq = jnp.sum(jnp.where(lane == LANE_F + h, cq_ref[0], 0.0), axis=-1, keepdims=True)
    m_sc[...] = jnp.full(m_sc.shape, NEG_BIG, F32)
    l_sc[...] = jnp.zeros_like(l_sc)
    acc_sc[...] = jnp.zeros_like(acc_sc)

    def step(j, masked):
        ks = pl.multiple_of(j * tq, tq)
        k = k_ref[0, pl.ds(ks, tq), :]
        v = v_ref[0, pl.ds(ks, tq), :]
        s = _dot_nt(q, k) * scale + (cq - ck_ref[0, pl.ds(h, 1), pl.ds(ks, tq)])
        if masked:
            row = lax.broadcasted_iota(jnp.int32, (tq, tq), 0)
            col = lax.broadcasted_iota(jnp.int32, (tq, tq), 1)
            s = jnp.where(row >= col, s, -jnp.inf)
        m_prev = m_sc[...]
        m_new = jnp.maximum(m_prev, jnp.max(s, axis=-1, keepdims=True))
        alpha = jnp.exp(m_prev - m_new)
        p = jnp.exp(s - m_new)
        l_sc[...] = alpha * l_sc[...] + jnp.sum(p, axis=-1, keepdims=True)
        acc_sc[...] = alpha * acc_sc[...] + _dot(p.astype(BF16), v)
        m_sc[...] = m_new

    def body(j, carry):
        step(j, False)
        return carry

    lax.fori_loop(0, i, body, 0)
    step(i, True)
    o_ref[0] = (acc_sc[...] / l_sc[...]).astype(BF16)


def _fox(main3, c3, ct3, tq):
    b, s, _ = main3.shape
    dh = FOX_HEAD_DIM
    return pl.pallas_call(
        functools.partial(_fox_kernel, tq=tq),
        grid=(b, FOX_HEADS, s // tq),
        in_specs=[
            pl.BlockSpec((1, tq, dh), lambda bi, h, i: (bi, i, COL_Q // dh + h)),
            pl.BlockSpec((1, s, dh), lambda bi, h, i: (bi, 0, COL_K // dh + h)),
            pl.BlockSpec((1, s, dh), lambda bi, h, i: (bi, 0, COL_V // dh + h)),
            pl.BlockSpec((1, tq, LANES), lambda bi, h, i: (bi, i, 0)),
            pl.BlockSpec((1, FOX_HEADS, s), lambda bi, h, i: (bi, LANE_F // FOX_HEADS, 0)),
        ],
        out_specs=pl.BlockSpec((1, tq, dh), lambda bi, h, i: (bi, i, h)),
        out_shape=jax.ShapeDtypeStruct((b, s, FOX_HEADS * dh), BF16),
        scratch_shapes=[pltpu.VMEM((tq, 1), F32), pltpu.VMEM((tq, 1), F32), pltpu.VMEM((tq, dh), F32)],
        compiler_params=_params(("arbitrary", "arbitrary", "arbitrary")),
        name="fox",
    )(main3, main3, main3, c3, ct3)


def _merge_kernel(ys_ref, yf_ref, gt_ref, bg_ref, x_ref, ws_ref, wf_ref, wo_ref, h_ref):
    d = x_ref.shape[1]
    g = _sigmoid(gt_ref[...].astype(F32) + bg_ref[...])
    merged = g[:, :d] * _dot(ys_ref[...], ws_ref[...]) + g[:, d:] * _dot(yf_ref[...], wf_ref[...])
    h_ref[...] = x_ref[...] + _dot(merged.astype(BF16), wo_ref[...])


def _merge(y_ssd, y_fox, main2, b_gates, x2, w_s, w_f, w_o, tm):
    n, d = x2.shape
    const = lambda shape: pl.BlockSpec(shape, lambda i: (0,) * len(shape))
    return pl.pallas_call(
        _merge_kernel,
        grid=(n // tm,),
        in_specs=[
            pl.BlockSpec((tm, d), lambda i: (i, 0)),
            pl.BlockSpec((tm, d), lambda i: (i, 0)),
            pl.BlockSpec((tm, 2 * d), lambda i: (i, COL_GATE // (2 * d))),
            const((1, 2 * d)),
            pl.BlockSpec((tm, d), lambda i: (i, 0)),
            const((d, d)), const((d, d)), const((d, d)),
        ],
        out_specs=pl.BlockSpec((tm, d), lambda i: (i, 0)),
        out_shape=jax.ShapeDtypeStruct((n, d), F32),
        compiler_params=_params(("arbitrary",)),
        name="merge",
    )(y_ssd, y_fox, main2, b_gates, x2, w_s, w_f, w_o)


def _memkv_kernel(mem_ref, g_ref, w_ref, kv_ref):
    kv_ref[0] = _dot(_rms(mem_ref[0], g_ref[...]).astype(BF16), w_ref[...]).astype(BF16)


def _memkv(mem, g, w_kv):
    b, m, d = mem.shape
    return pl.pallas_call(
        _memkv_kernel,
        grid=(b,),
        in_specs=[
            pl.BlockSpec((1, m, d), lambda i: (i, 0, 0)),
            pl.BlockSpec((1, d), lambda i: (0, 0)),
            pl.BlockSpec((d, 2 * d), lambda i: (0, 0)),
        ],
        out_specs=pl.BlockSpec((1, m, 2 * d), lambda i: (i, 0, 0)),
        out_shape=jax.ShapeDtypeStruct((b, m, 2 * d), BF16),
        compiler_params=_params(("arbitrary",)),
        name="memkv",
    )(mem, g, w_kv)


def _cross_kernel(h_ref, kv_ref, gc_ref, wq_ref, wo_ref, gf_ref, wr_ref, br_ref,
                  h2_ref, u3_ref, route_ref, cnt_ref, base):
    tm, d = h_ref.shape
    dh = d // X_HEADS

    @pl.when(pl.program_id(0) == 0)
    def _():
        base[...] = jnp.zeros_like(base)

    h1 = h_ref[...]
    q = _dot(_rms(h1, gc_ref[...]).astype(BF16), wq_ref[...]).astype(BF16)
    outs = []
    for hd in range(X_HEADS):
        kh = kv_ref[0, :, hd * dh:(hd + 1) * dh]
        vh = kv_ref[0, :, d + hd * dh:d + (hd + 1) * dh]
        s = _dot_nt(q[:, hd * dh:(hd + 1) * dh], kh) * (dh ** -0.5)
        p = jnp.exp(s - jnp.max(s, axis=-1, keepdims=True))
        outs.append(_dot(p.astype(BF16), vh) / jnp.sum(p, axis=-1, keepdims=True))
    h2 = h1 + _dot(jnp.concatenate(outs, axis=1).astype(BF16), wo_ref[...])
    h2_ref[...] = h2
    u3 = _rms(h2, gf_ref[...])
    u3_ref[...] = u3

    logits = jnp.dot(u3, wr_ref[...], precision=lax.Precision.HIGHEST, preferred_element_type=F32) + br_ref[...]
    lane = lax.broadcasted_iota(jnp.int32, (tm, LANES), 1)
    lane_f = lane.astype(F32)
    vals, idxs = [], []
    work = logits
    for _ in range(TOP_K):
        m = jnp.max(work, axis=-1, keepdims=True)
        ix = jnp.min(jnp.where(work == m, lane_f, float(LANES)), axis=-1, keepdims=True)
        vals.append(m)
        idxs.append(ix)
        work = jnp.where(lane_f == ix, -jnp.inf, work)
    exps = [jnp.exp(v - vals[0]) for v in vals]
    den = exps[0] + exps[1] + exps[2] + exps[3]

    onehot = jnp.zeros((tm, LANES), F32)
    for ix in idxs:
        onehot = onehot + jnp.where(lane_f == ix, 1.0, 0.0)
    row = lax.broadcasted_iota(jnp.int32, (tm, tm), 0)
    col = lax.broadcasted_iota(jnp.int32, (tm, tm), 1)
    strict = jnp.where(row > col, 1.0, 0.0).astype(BF16)
    rank = _dot(strict, onehot.astype(BF16)) + base[...]
    route = jnp.zeros((tm, LANES), F32)
    for r in range(TOP_K):
        pos = jnp.sum(jnp.where(lane_f == idxs[r], rank, 0.0), axis=-1, keepdims=True)
        route = jnp.where(lane == LANE_IDX + r, idxs[r], route)
        route = jnp.where(lane == LANE_POS + r, pos, route)
        route = jnp.where(lane == LANE_GATE + r, exps[r] / den, route)
    route_ref[...] = route
    new_base = base[...] + jnp.sum(onehot, axis=0, keepdims=True)
    base[...] = new_base
    cnt_ref[...] = new_base


def _cross(h1, kv, g_cross, w_q, w_o, g_ffn, w_r, b_r, tm, seq):
    n, d = h1.shape
    m = kv.shape[1]
    per_batch = seq // tm
    const = lambda shape: pl.BlockSpec(shape, lambda i: (0,) * len(shape))
    return pl.pallas_call(
        _cross_kernel,
        grid=(n // tm,),
        in_specs=[
            pl.BlockSpec((tm, d), lambda i: (i, 0)),
            pl.BlockSpec((1, m, 2 * d), lambda i: (i // per_batch, 0, 0)),
            const((1, d)), const((d, d)), const((d, d)), const((1, d)),
            const((d, LANES)), const((1, LANES)),
        ],
        out_specs=[
            pl.BlockSpec((tm, d), lambda i: (i, 0)),
            pl.BlockSpec((tm, d), lambda i: (i, 0)),
            pl.BlockSpec((tm, LANES), lambda i: (i, 0)),
            pl.BlockSpec((1, LANES), lambda i: (0, 0)),
        ],
        out_shape=[
            jax.ShapeDtypeStruct((n, d), F32),
            jax.ShapeDtypeStruct((n, d), F32),
            jax.ShapeDtypeStruct((n, LANES), F32),
            jax.ShapeDtypeStruct((1, LANES), F32),
        ],
        scratch_shapes=[pltpu.VMEM((1, LANES), F32)],
        compiler_params=_params(("arbitrary",)),
        name="cross_router",
    )(h1, kv, g_cross, w_q, w_o, g_ffn, w_r, b_r)


def _dispatch_kernel(zb_ref, dest_ref, u_ref, xr_ref, zbuf, sem, zsem, *, td):
    i = pl.program_id(0)

    @pl.when(i == 0)
    def _():
        zbuf[...] = jnp.zeros_like(zbuf)

        def zero_copy(e):
            return pltpu.make_async_copy(zbuf, xr_ref.at[pl.ds(zb_ref[e] * EXPERT_ROWS, EXPERT_ROWS)], zsem)

        for e in range(N_EXPERTS):
            @pl.when(zb_ref[e] >= 0)
            def _():
                zero_copy(e).start()
        for e in range(N_EXPERTS):
            @pl.when(zb_ref[e] >= 0)
            def _():
                zero_copy(e).wait()

    def row_copy(tok, dst):
        return pltpu.make_async_copy(u_ref.at[pl.ds(tok, 1)], xr_ref.at[pl.ds(dst, 1)], sem)

    def issue(t, carry):
        for k in range(TOP_K):
            row_copy(i * td + t, dest_ref[t * TOP_K + k]).start()
        return carry

    def drain(t, carry):
        for k in range(TOP_K):
            row_copy(0, 0).wait()
        return carry

    lax.fori_loop(0, td, issue, 0)
    lax.fori_loop(0, td, drain, 0)


def _dispatch(zero_blk, dest_flat, u3, n_rows, td):
    n, d = u3.shape
    return pl.pallas_call(
        functools.partial(_dispatch_kernel, td=td),
        grid_spec=pltpu.PrefetchScalarGridSpec(
            num_scalar_prefetch=1,
            grid=(n // td,),
            in_specs=[
                pl.BlockSpec((td * TOP_K,), lambda i, zb: (i,), memory_space=pltpu.SMEM),
                pl.BlockSpec(memory_space=pl.ANY),
            ],
            out_specs=pl.BlockSpec(memory_space=pl.ANY),
            scratch_shapes=[
                pltpu.VMEM((EXPERT_ROWS, d), F32),
                pltpu.SemaphoreType.DMA(()),
                pltpu.SemaphoreType.DMA(()),
            ],
        ),
        out_shape=jax.ShapeDtypeStruct((n_rows, d), F32),
        compiler_params=_params(("arbitrary",)),
        name="dispatch",
    )(zero_blk, dest_flat, u3)


def _expert_kernel(be_ref, nu_ref, x_ref, w1_ref, b1_ref, w2_ref, b2_ref, y_ref):
    f = w2_ref.shape[1]

    @pl.when(pl.program_id(0) < nu_ref[0])
    def _():
        hgl = _dot(x_ref[...].astype(BF16), w1_ref[0]) + b1_ref[0]
        glu = jnp.minimum(hgl[:, :f], SWIGLU_LIMIT)
        lin = jnp.clip(hgl[:, f:], -SWIGLU_LIMIT, SWIGLU_LIMIT)
        act = glu * _sigmoid(SWIGLU_ALPHA * glu) * (lin + 1.0)
        y_ref[...] = _dot(act.astype(BF16), w2_ref[0]) + b2_ref[0]


def _experts(blk_e, n_used, x_rows, w1, b1, w2, b2):
    n_rows, d = x_rows.shape
    f = w2.shape[1]
    nb = n_rows // EXPERT_ROWS
    row_blk = lambda i, be, nu: (jnp.minimum(i, nu[0] - 1), 0)
    return pl.pallas_call(
        _expert_kernel,
        grid_spec=pltpu.PrefetchScalarGridSpec(
            num_scalar_prefetch=2,
            grid=(nb,),
            in_specs=[
                pl.BlockSpec((EXPERT_ROWS, d), row_blk),
                pl.BlockSpec((1, d, 2 * f), lambda i, be, nu: (be[i], 0, 0)),
                pl.BlockSpec((1, 1, 2 * f), lambda i, be, nu: (be[i], 0, 0)),
                pl.BlockSpec((1, f, d), lambda i, be, nu: (be[i], 0, 0)),
                pl.BlockSpec((1, 1, d), lambda i, be, nu: (be[i], 0, 0)),
            ],
            out_specs=pl.BlockSpec((EXPERT_ROWS, d), row_blk),
        ),
        out_shape=jax.ShapeDtypeStruct((n_rows, d), F32),
        compiler_params=_params(("arbitrary",)),
        name="experts",
    )(blk_e, n_used, x_rows, w1, b1, w2, b2)


def _combine_kernel(dest_ref, y_ref, route_ref, h_ref, g_ref, o_ref, ybuf, sem, *, tc):
    def row_copy(src, k, t):
        return pltpu.make_async_copy(y_ref.at[pl.ds(src, 1)], ybuf.at[k, pl.ds(t, 1)], sem)

    def issue(t, carry):
        for k in range(TOP_K):
            row_copy(dest_ref[t * TOP_K + k], k, t).start()
        return carry

    def drain(t, carry):
        for k in range(TOP_K):
            row_copy(0, k, t).wait()
        return carry

    lax.fori_loop(0, tc, issue, 0)
    lax.fori_loop(0, tc, drain, 0)

    route = route_ref[...]
    lane = lax.broadcasted_iota(jnp.int32, route.shape, 1)
    acc = h_ref[...]
    for k in range(TOP_K):
        gate = jnp.sum(jnp.where(lane == LANE_GATE + k, route, 0.0), axis=-1, keepdims=True)
        acc = acc + gate * ybuf[k]
    o_ref[...] = _rms(acc, g_ref[...])


def _combine(dest_flat, y_rows, route, h2, g_final, tc):
    n, d = h2.shape
    return pl.pallas_call(
        functools.partial(_combine_kernel, tc=tc),
        grid=(n // tc,),
        in_specs=[
            pl.BlockSpec((tc * TOP_K,), lambda i: (i,), memory_space=pltpu.SMEM),
            pl.BlockSpec(memory_space=pl.ANY),
            pl.BlockSpec((tc, LANES), lambda i: (i, 0)),
            pl.BlockSpec((tc, d), lambda i: (i, 0)),
            pl.BlockSpec((1, d), lambda i: (0, 0)),
        ],
        out_specs=pl.BlockSpec((tc, d), lambda i: (i, 0)),
        out_shape=jax.ShapeDtypeStruct((n, d), F32),
        scratch_shapes=[pltpu.VMEM((TOP_K, tc, d), F32), pltpu.SemaphoreType.DMA(())],
        compiler_params=_params(("arbitrary",)),
        name="combine",
    )(dest_flat, y_rows, route, h2, g_final)


def _pad_lanes(v, offset=0, fill=0.0):
    out = jnp.full((1, LANES), fill, F32)
    return lax.dynamic_update_slice(out, v.reshape(1, -1).astype(F32), (0, offset))


def _layer(h, mem, norm_mix_g, w_in, conv_w, conv_b, dt_bias, a_log, d_skip, ssd_norm_g,
           w_ssd_branch, b_forget, w_fox_branch, b_gates, w_out, norm_cross_g, norm_mem_g,
           w_cross_q, w_cross_kv, w_cross_o, norm_ffn_g, w_router, b_router,
           w_expert_in, b_expert_in, w_expert_out, b_expert_out, final_g):
    b, s, d = h.shape
    n = b * s
    inner = SSD_HEADS * SSD_HEAD_DIM
    cc = inner + 2 * SSD_GROUPS * SSD_STATE
    tm = min(512, s)

    cuts = [inner, cc, SSD_HEADS, d, d, d, FOX_HEADS, 2 * d]
    offs = [0]
    for c in cuts:
        offs.append(offs[-1] + c)
    wz, wxbc, wdt, wq, wk, wv, wf, wg = [w_in[:, offs[i]:offs[i + 1]] for i in range(8)]
    w_main = jnp.concatenate([wxbc, wz, wq, wk, wv, wg], axis=1).astype(BF16)
    w_small = jnp.concatenate(
        [wdt, wf, jnp.zeros((d, LANES - SSD_HEADS - FOX_HEADS), F32)], axis=1).astype(BF16)

    x2 = h.reshape(n, d)
    main, small = _inproj(x2, norm_mix_g.reshape(1, d), w_main, w_small, min(1024, n), 1024)
    main3 = main.reshape(b, s, MAIN_COLS)

    bias_row = _pad_lanes(jnp.concatenate([dt_bias, b_forget]))
    alog_row = _pad_lanes(a_log)
    dskip_row = jnp.repeat(d_skip, SSD_HEAD_DIM).reshape(1, inner)
    e_mat = (jnp.arange(LANES)[:, None] == (jnp.arange(inner)[None, :] // SSD_HEAD_DIM)).astype(BF16)
    y_ssd, c3, ct3 = _ssd(main3, small.reshape(b, s, LANES), conv_w, conv_b.reshape(1, cc), bias_row,
                          alog_row, dskip_row, ssd_norm_g.reshape(1, inner), e_mat)
    y_fox = _fox(main3, c3, ct3, tm)

    h1 = _merge(y_ssd.reshape(n, inner), y_fox.reshape(n, d), main, b_gates.reshape(1, 2 * d), x2,
                w_ssd_branch.astype(BF16), w_fox_branch.astype(BF16), w_out.astype(BF16), tm)

    kv = _memkv(mem, norm_mem_g.reshape(1, d), w_cross_kv.astype(BF16))
    w_r = jnp.concatenate([w_router, jnp.zeros((d, LANES - N_EXPERTS), F32)], axis=1)
    b_r = _pad_lanes(b_router, fill=NEG_BIG)
    h2, u3, route, counts = _cross(h1, kv, norm_cross_g.reshape(1, d), w_cross_q.astype(BF16),
                                   w_cross_o.astype(BF16), norm_ffn_g.reshape(1, d), w_r, b_r, tm, s)

    idx = route[:, LANE_IDX:LANE_IDX + TOP_K].astype(jnp.int32)
    pos = route[:, LANE_POS:LANE_POS + TOP_K].astype(jnp.int32)
    cnt = counts[0, :N_EXPERTS].astype(jnp.int32)
    nblk = (cnt + EXPERT_ROWS - 1) // EXPERT_ROWS
    bend = jnp.cumsum(nblk)
    bstart = bend - nblk
    n_used = bend[-1:]
    dest = (bstart[idx] * EXPERT_ROWS + pos).reshape(n * TOP_K)
    nb = n * TOP_K // EXPERT_ROWS + N_EXPERTS
    blk = jnp.minimum(jnp.arange(nb, dtype=jnp.int32), n_used[0] - 1)
    blk_e = jnp.minimum(jnp.searchsorted(bend, blk, side='right'), N_EXPERTS - 1).astype(jnp.int32)
    zero_blk = jnp.where(nblk > 0, bend - 1, -1).astype(jnp.int32)

    td = min(256, n)
    x_rows = _dispatch(zero_blk, dest, u3, nb * EXPERT_ROWS, td)
    y_rows = _experts(blk_e, n_used.astype(jnp.int32), x_rows, w_expert_in.astype(BF16),
                      b_expert_in.reshape(N_EXPERTS, 1, -1), w_expert_out.astype(BF16),
                      b_expert_out.reshape(N_EXPERTS, 1, -1))
    out = _combine(dest, y_rows, route, h2, final_g.reshape(1, d), td)
    return out.reshape(b, s, d)


def kernel(x, mem, norm_mix_g, w_in, conv_w, conv_b, dt_bias, a_log, d_skip, ssd_norm_g, w_ssd_branch,
           b_forget, w_fox_branch, b_gates, w_out, norm_cross_g, norm_mem_g, w_cross_q, w_cross_kv,
           w_cross_o, norm_ffn_g, w_router, b_router, w_expert_in, b_expert_in, w_expert_out,
           b_expert_out, norm_final_g):
    assert x.shape[2] == FOX_HEADS * FOX_HEAD_DIM == SSD_HEADS * SSD_HEAD_DIM
    assert norm_mix_g.shape[0] == 1, "single-layer problem"
    l = 0
    return _layer(x, mem, norm_mix_g[l], w_in[l], conv_w[l], conv_b[l], dt_bias[l], a_log[l], d_skip[l],
                  ssd_norm_g[l], w_ssd_branch[l], b_forget[l], w_fox_branch[l], b_gates[l], w_out[l],
                  norm_cross_g[l], norm_mem_g[l], w_cross_q[l], w_cross_kv[l], w_cross_o[l],
                  norm_ffn_g[l], w_router[l], b_router[l], w_expert_in[l], b_expert_in[l],
                  w_expert_out[l], b_expert_out[l], norm_final_g)
```

```python
import functools

import jax
import jax.numpy as jnp
from jax import lax
from jax.experimental import pallas as pl
from jax.experimental.pallas import tpu as pltpu

F32 = jnp.float32
BF16 = jnp.bfloat16

NORM_EPS = 1e-6
SSD_HEAD_DIM = 64
SSD_HEADS = 16
SSD_GROUPS = 4
SSD_STATE = 128
SSD_CONV = 4
SSD_CHUNK = 128
FOX_HEADS = 8
FOX_HEAD_DIM = 128
X_HEADS = 4
N_EXPERTS = 32
TOP_K = 4
SWIGLU_ALPHA = 1.702
SWIGLU_LIMIT = 7.0

LANES = 128
CONV_HALO = 8
EXPERT_ROWS = 512
NEG_BIG = -1e30
VMEM_LIMIT = 56 * 1024 * 1024

COL_XBC, COL_Z, COL_Q, COL_K, COL_V, COL_GATE, MAIN_COLS = 0, 2048, 3072, 4096, 5120, 6144, 8192
LANE_DT, LANE_F = 0, 16
LANE_IDX, LANE_POS, LANE_GATE = 0, 4, 8


def _sigmoid(x):
    return 1.0 / (1.0 + jnp.exp(-x))


def _rms(x, g):
    ms = jnp.mean(x * x, axis=-1, keepdims=True)
    return x * lax.rsqrt(ms + NORM_EPS) * g


def _dot(a, b):
    return jnp.dot(a, b, preferred_element_type=F32)


def _dot_nt(a, b):
    return lax.dot_general(a, b, (((1,), (1,)), ((), ())), preferred_element_type=F32)


def _split_dot(v, m_bf16, terms):
    out = None
    for _ in range(terms):
        hi = v.astype(BF16)
        part = _dot(hi, m_bf16)
        out = part if out is None else out + part
        v = v - hi.astype(F32)
    return out


def _params(sem, vmem=VMEM_LIMIT):
    return pltpu.CompilerParams(dimension_semantics=sem, vmem_limit_bytes=vmem)


def _inproj_kernel(x_ref, g_ref, wm_ref, ws_ref, main_ref, small_ref, u_ref):
    @pl.when(pl.program_id(1) == 0)
    def _():
        u = _rms(x_ref[...], g_ref[...]).astype(BF16)
        u_ref[...] = u
        small_ref[...] = _dot(u, ws_ref[...])

    main_ref[...] = _dot(u_ref[...], wm_ref[...]).astype(BF16)


def _inproj(x2, g, w_main, w_small, tm, tn):
    n, d = x2.shape
    return pl.pallas_call(
        _inproj_kernel,
        grid=(n // tm, MAIN_COLS // tn),
        in_specs=[
            pl.BlockSpec((tm, d), lambda i, j: (i, 0)),
            pl.BlockSpec((1, d), lambda i, j: (0, 0)),
            pl.BlockSpec((d, tn), lambda i, j: (0, j)),
            pl.BlockSpec((d, LANES), lambda i, j: (0, 0)),
        ],
        out_specs=[
            pl.BlockSpec((tm, tn), lambda i, j: (i, j)),
            pl.BlockSpec((tm, LANES), lambda i, j: (i, 0)),
        ],
        out_shape=[jax.ShapeDtypeStruct((n, MAIN_COLS), BF16), jax.ShapeDtypeStruct((n, LANES), F32)],
        scratch_shapes=[pltpu.VMEM((tm, d), BF16)],
        compiler_params=_params(("arbitrary", "arbitrary")),
        name="inproj",
    )(x2, g, w_main, w_small)


def _ssd_kernel(xbc_ref, z_ref, sm_ref, cw_ref, cb_ref, bias_ref, alog_ref, dsk_ref, ng_ref, e_ref,
                y_ref, ct_ref, state, buf, ccarry):
    L = SSD_CHUNK
    inner = SSD_HEADS * SSD_HEAD_DIM
    gw = inner // SSD_GROUPS
    hpg = SSD_HEADS // SSD_GROUPS

    @pl.when(pl.program_id(1) == 0)
    def _():
        state[...] = jnp.zeros_like(state)
        buf[0:CONV_HALO, :] = jnp.zeros((CONV_HALO, buf.shape[1]), F32)
        ccarry[...] = jnp.zeros_like(ccarry)

    xbc = xbc_ref[0].astype(F32)
    buf[CONV_HALO:CONV_HALO + L, :] = xbc
    acc = cb_ref[...] + cw_ref[SSD_CONV - 1:SSD_CONV, :] * xbc
    for j in range(SSD_CONV - 1):
        acc = acc + cw_ref[j:j + 1, :] * buf[pl.ds(CONV_HALO - (SSD_CONV - 1) + j, L), :]
    buf[0:CONV_HALO, :] = xbc[L - CONV_HALO:L, :]
    xc = acc * _sigmoid(acc)
    xs = xc[:, :inner]
    bm = xc[:, inner:inner + SSD_GROUPS * SSD_STATE]
    cm = xc[:, inner + SSD_GROUPS * SSD_STATE:]

    sm = sm_ref[0] + bias_ref[...]
    t = jnp.log1p(jnp.exp(-jnp.abs(sm)))
    dt = jnp.maximum(sm, 0.0) + t
    logf = jnp.minimum(sm, 0.0) - t
    lane = lax.broadcasted_iota(jnp.int32, (L, LANES), 1)
    is_dt = lane < LANE_F
    a = -jnp.exp(alog_ref[...]) * dt
    comb = jnp.where(is_dt, a, jnp.where(lane < LANE_F + FOX_HEADS, logf, 0.0))
    row = lax.broadcasted_iota(jnp.int32, (L, L), 0)
    col = lax.broadcasted_iota(jnp.int32, (L, L), 1)
    causal = row >= col
    tril = jnp.where(causal, 1.0, 0.0).astype(BF16)
    cum = _split_dot_left(tril, comb)
    cglob = cum + ccarry[...]
    ccarry[...] = cglob[L - 1:L, :]
    ct_ref[0] = cglob.T

    acum = jnp.where(is_dt, cum, 0.0)
    act = acum.T
    alast = acum[L - 1:L, :]
    e_mat = e_ref[...]
    dt_e = _split_dot(jnp.where(is_dt, dt, 0.0), e_mat, 2)
    ds_e = _split_dot(jnp.where(is_dt, jnp.exp(alast - acum), 0.0), e_mat, 2)
    ea_e = _split_dot(jnp.where(is_dt, jnp.exp(acum), 0.0), e_mat, 2)
    cd_e = _split_dot(jnp.broadcast_to(jnp.where(is_dt[0:1], jnp.exp(alast), 0.0), (8, LANES)), e_mat, 2)[0:1]

    xdt = xs * dt_e
    xdt_b = xdt.astype(BF16)
    xds_b = (xdt * ds_e).astype(BF16)
    glane = lax.broadcasted_iota(jnp.int32, (L, gw), 1)
    ys = []
    for g in range(SSD_GROUPS):
        gs = slice(g * gw, (g + 1) * gw)
        bg = bm[:, g * SSD_STATE:(g + 1) * SSD_STATE]
        cg = cm[:, g * SSD_STATE:(g + 1) * SSD_STATE].astype(BF16)
        cb = _dot_nt(cg, bg.astype(BF16))
        xg = xdt_b[:, gs]
        ydiag = jnp.zeros((L, gw), F32)
        for e in range(hpg):
            h = g * hpg + e
            diff = acum[:, h:h + 1] - act[h:h + 1, :]
            dm = jnp.exp(jnp.where(causal, diff, -jnp.inf))
            yfull = _dot((cb * dm).astype(BF16), xg)
            ydiag = jnp.where((glane >= e * SSD_HEAD_DIM) & (glane < (e + 1) * SSD_HEAD_DIM), yfull, ydiag)
        stg = state[:, gs]
        yoff = _dot(cg, stg.astype(BF16)) * ea_e[:, gs]
        state[:, gs] = stg * cd_e[:, gs] + _dot(bg.T.astype(BF16), xds_b[:, gs])
        ys.append(ydiag + yoff)
    y = jnp.concatenate(ys, axis=1) + dsk_ref[...] * xs

    z = z_ref[0].astype(F32)
    gated = y * (z * _sigmoid(z))
    outs = []
    for g in range(SSD_GROUPS):
        seg = gated[:, g * gw:(g + 1) * gw]
        ms = jnp.mean(seg * seg, axis=-1, keepdims=True)
        outs.append(seg * lax.rsqrt(ms + NORM_EPS))
    y_ref[0] = (jnp.concatenate(outs, axis=1) * ng_ref[...]).astype(BF16)


def _split_dot_left(m_bf16, v):
    out = None
    for _ in range(3):
        hi = v.astype(BF16)
        part = _dot(m_bf16, hi)
        out = part if out is None else out + part
        v = v - hi.astype(F32)
    return out


def _ssd(main3, small3, conv_w, conv_b, bias_row, alog_row, dskip_row, normg_row, e_mat):
    b, s, _ = main3.shape
    L = SSD_CHUNK
    inner = SSD_HEADS * SSD_HEAD_DIM
    cc = conv_w.shape[1]
    const = lambda shape: pl.BlockSpec(shape, lambda i, c: (0,) * len(shape))
    return pl.pallas_call(
        _ssd_kernel,
        grid=(b, s // L),
        in_specs=[
            pl.BlockSpec((1, L, cc), lambda i, c: (i, c, COL_XBC // cc)),
            pl.BlockSpec((1, L, inner), lambda i, c: (i, c, COL_Z // inner)),
            pl.BlockSpec((1, L, LANES), lambda i, c: (i, c, 0)),
            const((SSD_CONV, cc)), const((1, cc)), const((1, LANES)), const((1, LANES)),
            const((1, inner)), const((1, inner)), const((LANES, inner)),
        ],
        out_specs=[
            pl.BlockSpec((1, L, inner), lambda i, c: (i, c, 0)),
            pl.BlockSpec((1, LANES, L), lambda i, c: (i, 0, c)),
        ],
        out_shape=[
            jax.ShapeDtypeStruct((b, s, inner), BF16),
            jax.ShapeDtypeStruct((b, LANES, s), F32),
        ],
        scratch_shapes=[
            pltpu.VMEM((SSD_STATE, inner), F32),
            pltpu.VMEM((L + CONV_HALO, cc), F32),
            pltpu.VMEM((1, LANES), F32),
        ],
        compiler_params=_params(("arbitrary", "arbitrary")),
        name="ssd",
    )(main3, main3, small3, conv_w, conv_b, bias_row, alog_row, dskip_row, normg_row, e_mat)


def _fox_kernel(q_ref, k_ref, v_ref, ck_ref, o_ref, m_sc, l_sc, acc_sc, *, tq):
    h = pl.program_id(1)
    i = pl.program_id(2)
    q = q_ref[0]
    m_sc[...] = jnp.full(m_sc.shape, NEG_BIG, F32)
    l_sc[...] = jnp.zeros_like(l_sc)
    acc_sc[...] = jnp.zeros_like(acc_sc)
    reps = tq // LANES

    def step(j, masked):
        ks = pl.multiple_of(j * tq, tq)
        k = k_ref[0, pl.ds(ks, tq), :]
        v = v_ref[0, pl.ds(ks, tq), :]
        s = _dot_nt(q, k) - ck_ref[0, pl.ds(h, 1), pl.ds(ks, tq)]
        if masked:
            row = lax.broadcasted_iota(jnp.int32, (tq, tq), 0)
            col = lax.broadcasted_iota(jnp.int32, (tq, tq), 1)
            s = jnp.where(row >= col, s, -jnp.inf)
        m_prev = m_sc[...]
        m_new = jnp.maximum(m_prev, jnp.max(s, axis=-1, keepdims=True))
        alpha = jnp.exp(m_prev - m_new)
        p = jnp.exp(s - jnp.concatenate([m_new] * reps, axis=1))
        l_sc[...] = alpha * l_sc[...] + jnp.sum(p, axis=-1, keepdims=True)
        acc_sc[...] = alpha * acc_sc[...] + _dot(p.astype(BF16), v)
        m_sc[...] = m_new

    def body(j, carry):
        step(j, False)
        return carry

    lax.fori_loop(0, i, body, 0)
    step(i, True)
    o_ref[0] = (acc_sc[...] / l_sc[...]).astype(BF16)


def _fox(main3, ct3, tq):
    b, s, _ = main3.shape
    dh = FOX_HEAD_DIM
    assert dh == LANES
    return pl.pallas_call(
        functools.partial(_fox_kernel, tq=tq),
        grid=(b, FOX_HEADS, s // tq),
        in_specs=[
            pl.BlockSpec((1, tq, dh), lambda bi, h, i: (bi, i, COL_Q // dh + h)),
            pl.BlockSpec((1, s, dh), lambda bi, h, i: (bi, 0, COL_K // dh + h)),
            pl.BlockSpec((1, s, dh), lambda bi, h, i: (bi, 0, COL_V // dh + h)),
            pl.BlockSpec((1, FOX_HEADS, s), lambda bi, h, i: (bi, LANE_F // FOX_HEADS, 0)),
        ],
        out_specs=pl.BlockSpec((1, tq, dh), lambda bi, h, i: (bi, i, h)),
        out_shape=jax.ShapeDtypeStruct((b, s, FOX_HEADS * dh), BF16),
        scratch_shapes=[pltpu.VMEM((tq, LANES), F32), pltpu.VMEM((tq, LANES), F32), pltpu.VMEM((tq, dh), F32)],
        compiler_params=_params(("arbitrary", "arbitrary", "arbitrary")),
        name="fox",
    )(main3, main3, main3, ct3)


def _merge_kernel(ys_ref, yf_ref, gt_ref, bg_ref, x_ref, ws_ref, wf_ref, wo_ref, h_ref):
    d = x_ref.shape[1]
    g = _sigmoid(gt_ref[...].astype(F32) + bg_ref[...])
    merged = g[:, :d] * _dot(ys_ref[...], ws_ref[...]) + g[:, d:] * _dot(yf_ref[...], wf_ref[...])
    h_ref[...] = x_ref[...] + _dot(merged.astype(BF16), wo_ref[...])


def _merge(y_ssd, y_fox, main2, b_gates, x2, w_s, w_f, w_o, tm):
    n, d = x2.shape
    const = lambda shape: pl.BlockSpec(shape, lambda i: (0,) * len(shape))
    return pl.pallas_call(
        _merge_kernel,
        grid=(n // tm,),
        in_specs=[
            pl.BlockSpec((tm, d), lambda i: (i, 0)),
            pl.BlockSpec((tm, d), lambda i: (i, 0)),
            pl.BlockSpec((tm, 2 * d), lambda i: (i, COL_GATE // (2 * d))),
            const((1, 2 * d)),
            pl.BlockSpec((tm, d), lambda i: (i, 0)),
            const((d, d)), const((d, d)), const((d, d)),
        ],
        out_specs=pl.BlockSpec((tm, d), lambda i: (i, 0)),
        out_shape=jax.ShapeDtypeStruct((n, d), F32),
        compiler_params=_params(("arbitrary",)),
        name="merge",
    )(y_ssd, y_fox, main2, b_gates, x2, w_s, w_f, w_o)


def _memkv_kernel(mem_ref, g_ref, w_ref, kv_ref):
    kv_ref[0] = _dot(_rms(mem_ref[0], g_ref[...]).astype(BF16), w_ref[...]).astype(BF16)


def _memkv(mem, g, w_kv):
    b, m, d = mem.shape
    return pl.pallas_call(
        _memkv_kernel,
        grid=(b,),
        in_specs=[
            pl.BlockSpec((1, m, d), lambda i: (i, 0, 0)),
            pl.BlockSpec((1, d), lambda i: (0, 0)),
            pl.BlockSpec((d, 2 * d), lambda i: (0, 0)),
        ],
        out_specs=pl.BlockSpec((1, m, 2 * d), lambda i: (i, 0, 0)),
        out_shape=jax.ShapeDtypeStruct((b, m, 2 * d), BF16),
        compiler_params=_params(("arbitrary",)),
        name="memkv",
    )(mem, g, w_kv)


def _cross_kernel(h_ref, kv_ref, gc_ref, wq_ref, wo_ref, gf_ref, wr_ref, br_ref,
                  h2_ref, u3_ref, route_ref, cnt_ref, base):
    tm, d = h_ref.shape
    dh = d // X_HEADS

    @pl.when(pl.program_id(0) == 0)
    def _():
        base[...] = jnp.zeros_like(base)

    h1 = h_ref[...]
    q = _dot(_rms(h1, gc_ref[...]).astype(BF16), wq_ref[...]).astype(BF16)
    outs = []
    for hd in range(X_HEADS):
        kh = kv_ref[0, :, hd * dh:(hd + 1) * dh]
        vh = kv_ref[0, :, d + hd * dh:d + (hd + 1) * dh]
        s = _dot_nt(q[:, hd * dh:(hd + 1) * dh], kh) * (dh ** -0.5)
        p = jnp.exp(s - jnp.max(s, axis=-1, keepdims=True))
        outs.append(_dot(p.astype(BF16), vh) / jnp.sum(p, axis=-1, keepdims=True))
    h2 = h1 + _dot(jnp.concatenate(outs, axis=1).astype(BF16), wo_ref[...])
    h2_ref[...] = h2
    u3 = _rms(h2, gf_ref[...])
    u3_ref[...] = u3

    u_hi = u3.astype(BF16)
    u_lo = (u3 - u_hi.astype(F32)).astype(BF16)
    w_r = wr_ref[...]
    w_hi = w_r.astype(BF16)
    w_lo = (w_r - w_hi.astype(F32)).astype(BF16)
    logits = _dot(u_hi, w_hi) + _dot(u_hi, w_lo) + _dot(u_lo, w_hi) + br_ref[...]
    lane = lax.broadcasted_iota(jnp.int32, (tm, LANES), 1)
    lane_f = lane.astype(F32)
    vals, idxs = [], []
    work = logits
    for _ in range(TOP_K):
        m = jnp.max(work, axis=-1, keepdims=True)
        ix = jnp.min(jnp.where(work == m, lane_f, float(LANES)), axis=-1, keepdims=True)
        vals.append(m)
        idxs.append(ix)
        work = jnp.where(lane_f == ix, -jnp.inf, work)
    exps = [jnp.exp(v - vals[0]) for v in vals]
    den = exps[0] + exps[1] + exps[2] + exps[3]

    onehot = jnp.zeros((tm, LANES), F32)
    for ix in idxs:
        onehot = onehot + jnp.where(lane_f == ix, 1.0, 0.0)
    row = lax.broadcasted_iota(jnp.int32, (tm, tm), 0)
    col = lax.broadcasted_iota(jnp.int32, (tm, tm), 1)
    strict = jnp.where(row > col, 1.0, 0.0).astype(BF16)
    rank = _dot(strict, onehot.astype(BF16)) + base[...]
    route = jnp.zeros((tm, LANES), F32)
    for r in range(TOP_K):
        pos = jnp.sum(jnp.where(lane_f == idxs[r], rank, 0.0), axis=-1, keepdims=True)
        route = jnp.where(lane == LANE_IDX + r, idxs[r], route)
        route = jnp.where(lane == LANE_POS + r, pos, route)
        route = jnp.where(lane == LANE_GATE + r, exps[r] / den, route)
    route_ref[...] = route
    new_base = base[...] + jnp.sum(onehot, axis=0, keepdims=True)
    base[...] = new_base
    cnt_ref[...] = new_base


def _cross(h1, kv, g_cross, w_q, w_o, g_ffn, w_r, b_r, tm, seq):
    n, d = h1.shape
    m = kv.shape[1]
    per_batch = seq // tm
    const = lambda shape: pl.BlockSpec(shape, lambda i: (0,) * len(shape))
    return pl.pallas_call(
        _cross_kernel,
        grid=(n // tm,),
        in_specs=[
            pl.BlockSpec((tm, d), lambda i: (i, 0)),
            pl.BlockSpec((1, m, 2 * d), lambda i: (i // per_batch, 0, 0)),
            const((1, d)), const((d, d)), const((d, d)), const((1, d)),
            const((d, LANES)), const((1, LANES)),
        ],
        out_specs=[
            pl.BlockSpec((tm, d), lambda i: (i, 0)),
            pl.BlockSpec((tm, d), lambda i: (i, 0)),
            pl.BlockSpec((tm, LANES), lambda i: (i, 0)),
            pl.BlockSpec((1, LANES), lambda i: (0, 0)),
        ],
        out_shape=[
            jax.ShapeDtypeStruct((n, d), F32),
            jax.ShapeDtypeStruct((n, d), F32),
            jax.ShapeDtypeStruct((n, LANES), F32),
            jax.ShapeDtypeStruct((1, LANES), F32),
        ],
        scratch_shapes=[pltpu.VMEM((1, LANES), F32)],
        compiler_params=_params(("arbitrary",)),
        name="cross_router",
    )(h1, kv, g_cross, w_q, w_o, g_ffn, w_r, b_r)


def _dispatch_kernel(zb_ref, dest_ref, u_ref, xr_ref, zbuf, sem, zsem, *, td):
    i = pl.program_id(0)

    @pl.when(i == 0)
    def _():
        zbuf[...] = jnp.zeros_like(zbuf)

        def zero_copy(e):
            return pltpu.make_async_copy(zbuf, xr_ref.at[pl.ds(zb_ref[e] * EXPERT_ROWS, EXPERT_ROWS)], zsem)

        for e in range(2 * N_EXPERTS):
            @pl.when(zb_ref[e] >= 0)
            def _():
                zero_copy(e).start()
        for e in range(2 * N_EXPERTS):
            @pl.when(zb_ref[e] >= 0)
            def _():
                zero_copy(e).wait()

    def issue(t, carry):
        for k in range(TOP_K):
            pltpu.make_async_copy(u_ref.at[pl.ds(t, 1)], xr_ref.at[pl.ds(dest_ref[t * TOP_K + k], 1)],
                                  sem).start(priority=k % 2)
        return carry

    lax.fori_loop(0, td, issue, 0, unroll=2)
    for k in range(TOP_K):
        pltpu.make_async_copy(u_ref, xr_ref.at[pl.ds(0, td)], sem).wait()


def _dispatch(zero_blk, dest_flat, u3, n_rows, td):
    n, d = u3.shape
    return pl.pallas_call(
        functools.partial(_dispatch_kernel, td=td),
        grid_spec=pltpu.PrefetchScalarGridSpec(
            num_scalar_prefetch=1,
            grid=(n // td,),
            in_specs=[
                pl.BlockSpec((td * TOP_K,), lambda i, zb: (i,), memory_space=pltpu.SMEM),
                pl.BlockSpec((td, d), lambda i, zb: (i, 0)),
            ],
            out_specs=pl.BlockSpec(memory_space=pl.ANY),
            scratch_shapes=[
                pltpu.VMEM((EXPERT_ROWS, d), F32),
                pltpu.SemaphoreType.DMA(()),
                pltpu.SemaphoreType.DMA(()),
            ],
        ),
        out_shape=jax.ShapeDtypeStruct((n_rows, d), F32),
        compiler_params=_params(("arbitrary",)),
        name="dispatch",
    )(zero_blk, dest_flat, u3)


def _expert_kernel(be_ref, nu_ref, x_ref, w1_ref, b1_ref, w2_ref, b2_ref, y_ref):
    f = w2_ref.shape[1]

    used = pl.program_id(0) < nu_ref[0]

    @pl.when(used)
    def _():
        hgl = _dot(x_ref[...].astype(BF16), w1_ref[0]) + b1_ref[0]
        glu = jnp.minimum(hgl[:, :f], SWIGLU_LIMIT)
        lin = jnp.clip(hgl[:, f:], -SWIGLU_LIMIT, SWIGLU_LIMIT)
        act = glu * _sigmoid(SWIGLU_ALPHA * glu) * (lin + 1.0)
        y_ref[...] = _dot(act.astype(BF16), w2_ref[0]) + b2_ref[0]

    @pl.when(jnp.logical_not(used))
    def _():
        y_ref[...] = jnp.zeros_like(y_ref)


def _experts(blk_e, n_used, x_rows, w1, b1, w2, b2):
    n_rows, d = x_rows.shape
    f = w2.shape[1]
    nb = n_rows // EXPERT_ROWS
    row_blk = lambda i, be, nu: (jnp.minimum(i, nu[0] - 1), 0)
    return pl.pallas_call(
        _expert_kernel,
        grid_spec=pltpu.PrefetchScalarGridSpec(
            num_scalar_prefetch=2,
            grid=(nb,),
            in_specs=[
                pl.BlockSpec((EXPERT_ROWS, d), row_blk),
                pl.BlockSpec((1, d, 2 * f), lambda i, be, nu: (be[i], 0, 0)),
                pl.BlockSpec((1, 1, 2 * f), lambda i, be, nu: (be[i], 0, 0)),
                pl.BlockSpec((1, f, d), lambda i, be, nu: (be[i], 0, 0)),
                pl.BlockSpec((1, 1, d), lambda i, be, nu: (be[i], 0, 0)),
            ],
            out_specs=pl.BlockSpec((EXPERT_ROWS, d), lambda i, be, nu: (i, 0)),
        ),
        out_shape=jax.ShapeDtypeStruct((n_rows, d), F32),
        compiler_params=_params(("arbitrary",)),
        name="experts",
    )(blk_e, n_used, x_rows, w1, b1, w2, b2)


def _combine_kernel(dest_ref, y_ref, route_ref, h_ref, g_ref, o_ref, ybuf, sem, *, tc):
    def issue(t, carry):
        for k in range(TOP_K):
            pltpu.make_async_copy(y_ref.at[pl.ds(dest_ref[t * TOP_K + k], 1)], ybuf.at[k, pl.ds(t, 1)],
                                  sem).start(priority=k % 2)
        return carry

    lax.fori_loop(0, tc, issue, 0, unroll=2)
    for k in range(TOP_K):
        pltpu.make_async_copy(y_ref.at[pl.ds(0, tc)], ybuf.at[k], sem).wait()

    route = route_ref[...]
    lane = lax.broadcasted_iota(jnp.int32, route.shape, 1)
    acc = h_ref[...]
    for k in range(TOP_K):
        gate = jnp.sum(jnp.where(lane == LANE_GATE + k, route, 0.0), axis=-1, keepdims=True)
        acc = acc + gate * ybuf[k]
    o_ref[...] = _rms(acc, g_ref[...])


def _combine(dest_flat, y_rows, route, h2, g_final, tc):
    n, d = h2.shape
    return pl.pallas_call(
        functools.partial(_combine_kernel, tc=tc),
        grid=(n // tc,),
        in_specs=[
            pl.BlockSpec((tc * TOP_K,), lambda i: (i,), memory_space=pltpu.SMEM),
            pl.BlockSpec(memory_space=pl.ANY),
            pl.BlockSpec((tc, LANES), lambda i: (i, 0)),
            pl.BlockSpec((tc, d), lambda i: (i, 0)),
            pl.BlockSpec((1, d), lambda i: (0, 0)),
        ],
        out_specs=pl.BlockSpec((tc, d), lambda i: (i, 0)),
        out_shape=jax.ShapeDtypeStruct((n, d), F32),
        scratch_shapes=[pltpu.VMEM((TOP_K, tc, d), F32), pltpu.SemaphoreType.DMA(())],
        compiler_params=_params(("arbitrary",)),
        name="combine",
    )(dest_flat, y_rows, route, h2, g_final)


def _pad_lanes(v, offset=0, fill=0.0):
    out = jnp.full((1, LANES), fill, F32)
    return lax.dynamic_update_slice(out, v.reshape(1, -1).astype(F32), (0, offset))


def _layer(h, mem, norm_mix_g, w_in, conv_w, conv_b, dt_bias, a_log, d_skip, ssd_norm_g,
           w_ssd_branch, b_forget, w_fox_branch, b_gates, w_out, norm_cross_g, norm_mem_g,
           w_cross_q, w_cross_kv, w_cross_o, norm_ffn_g, w_router, b_router,
           w_expert_in, b_expert_in, w_expert_out, b_expert_out, final_g):
    b, s, d = h.shape
    n = b * s
    inner = SSD_HEADS * SSD_HEAD_DIM
    cc = inner + 2 * SSD_GROUPS * SSD_STATE
    tm = min(512, s)

    cuts = [inner, cc, SSD_HEADS, d, d, d, FOX_HEADS, 2 * d]
    offs = [0]
    for c in cuts:
        offs.append(offs[-1] + c)
    wz, wxbc, wdt, wq, wk, wv, wf, wg = [w_in[:, offs[i]:offs[i + 1]] for i in range(8)]
    w_main = jnp.concatenate([wxbc, wz, wq * (FOX_HEAD_DIM ** -0.5), wk, wv, wg], axis=1).astype(BF16)
    w_small = jnp.concatenate(
        [wdt, wf, jnp.zeros((d, LANES - SSD_HEADS - FOX_HEADS), F32)], axis=1).astype(BF16)

    x2 = h.reshape(n, d)
    main, small = _inproj(x2, norm_mix_g.reshape(1, d), w_main, w_small, min(1024, n), 1024)
    main3 = main.reshape(b, s, MAIN_COLS)

    bias_row = _pad_lanes(jnp.concatenate([dt_bias, b_forget]))
    alog_row = _pad_lanes(a_log)
    dskip_row = jnp.repeat(d_skip, SSD_HEAD_DIM).reshape(1, inner)
    e_mat = (jnp.arange(LANES)[:, None] == (jnp.arange(inner)[None, :] // SSD_HEAD_DIM)).astype(BF16)
    y_ssd, ct3 = _ssd(main3, small.reshape(b, s, LANES), conv_w, conv_b.reshape(1, cc), bias_row,
                      alog_row, dskip_row, ssd_norm_g.reshape(1, inner), e_mat)
    y_fox = _fox(main3, ct3, tm)

    h1 = _merge(y_ssd.reshape(n, inner), y_fox.reshape(n, d), main, b_gates.reshape(1, 2 * d), x2,
                w_ssd_branch.astype(BF16), w_fox_branch.astype(BF16), w_out.astype(BF16), tm)

    kv = _memkv(mem, norm_mem_g.reshape(1, d), w_cross_kv.astype(BF16))
    w_r = jnp.concatenate([w_router, jnp.zeros((d, LANES - N_EXPERTS), F32)], axis=1)
    b_r = _pad_lanes(b_router, fill=NEG_BIG)
    h2, u3, route, counts = _cross(h1, kv, norm_cross_g.reshape(1, d), w_cross_q.astype(BF16),
                                   w_cross_o.astype(BF16), norm_ffn_g.reshape(1, d), w_r, b_r, tm, s)

    idx = route[:, LANE_IDX:LANE_IDX + TOP_K].astype(jnp.int32)
    pos = route[:, LANE_POS:LANE_POS + TOP_K].astype(jnp.int32)
    cnt = counts[0, :N_EXPERTS].astype(jnp.int32)
    nblk = (cnt + EXPERT_ROWS - 1) // EXPERT_ROWS
    bend = jnp.cumsum(nblk)
    bstart = bend - nblk
    n_used = bend[-1:]
    dest = (bstart[idx] * EXPERT_ROWS + pos).reshape(n * TOP_K)
    nb = n * TOP_K // EXPERT_ROWS + N_EXPERTS
    blk = jnp.minimum(jnp.arange(nb, dtype=jnp.int32), n_used[0] - 1)
    blk_e = jnp.minimum(jnp.sum(blk[:, None] >= bend[None, :], axis=1), N_EXPERTS - 1).astype(jnp.int32)
    tail = n_used[0] + jnp.arange(N_EXPERTS, dtype=jnp.int32)
    zero_blk = jnp.concatenate([jnp.where(nblk > 0, bend - 1, -1),
                                jnp.where(tail < nb, tail, -1)]).astype(jnp.int32)

    td = min(256, n)
    x_rows = _dispatch(zero_blk, dest, u3, nb * EXPERT_ROWS, td)
    y_rows = _experts(blk_e, n_used.astype(jnp.int32), x_rows, w_expert_in.astype(BF16),
                      b_expert_in.reshape(N_EXPERTS, 1, -1), w_expert_out.astype(BF16),
                      b_expert_out.reshape(N_EXPERTS, 1, -1))
    out = _combine(dest, y_rows, route, h2, final_g.reshape(1, d), td)
    return out.reshape(b, s, d)


def kernel(x, mem, norm_mix_g, w_in, conv_w, conv_b, dt_bias, a_log, d_skip, ssd_norm_g, w_ssd_branch,
           b_forget, w_fox_branch, b_gates, w_out, norm_cross_g, norm_mem_g, w_cross_q, w_cross_kv,
           w_cross_o, norm_ffn_g, w_router, b_router, w_expert_in, b_expert_in, w_expert_out,
           b_expert_out, norm_final_g):
    assert x.shape[2] == FOX_HEADS * FOX_HEAD_DIM == SSD_HEADS * SSD_HEAD_DIM
    assert norm_mix_g.shape[0] == 1, "single-layer problem"
    l = 0
    return _layer(x, mem, norm_mix_g[l], w_in[l], conv_w[l], conv_b[l], dt_bias[l], a_log[l], d_skip[l],
                  ssd_norm_g[l], w_ssd_branch[l], b_forget[l], w_fox_branch[l], b_gates[l], w_out[l],
                  norm_cross_g[l], norm_mem_g[l], w_cross_q[l], w_cross_kv[l], w_cross_o[l],
                  norm_ffn_g[l], w_router[l], b_router[l], w_expert_in[l], b_expert_in[l],
                  w_expert_out[l], b_expert_out[l], norm_final_g)
```

```python
import functools

import jax
import jax.numpy as jnp
from jax import lax
from jax.experimental import pallas as pl
from jax.experimental.pallas import tpu as pltpu

F32 = jnp.float32
BF16 = jnp.bfloat16

NORM_EPS = 1e-6
SSD_HEAD_DIM = 64
SSD_HEADS = 16
SSD_GROUPS = 4
SSD_STATE = 128
SSD_CONV = 4
SSD_CHUNK = 128
FOX_HEADS = 8
FOX_HEAD_DIM = 128
X_HEADS = 4
N_EXPERTS = 32
TOP_K = 4
SWIGLU_ALPHA = 1.702
SWIGLU_LIMIT = 7.0

LANES = 128
CONV_HALO = 8
ROW_SUB = 8
EXPERT_ROWS = 512
NEG_BIG = -1e30
LOG2E = 1.4426950408889634
FOX_TQ, FOX_TK = 1024, 512
VMEM_LIMIT = 56 * 1024 * 1024
EXPERT_VMEM_LIMIT = 60 * 1024 * 1024

COL_XBC, COL_Z, COL_Q, COL_K, COL_V, COL_GATE, MAIN_COLS = 0, 2048, 3072, 4096, 5120, 6144, 8192
LANE_DT, LANE_F = 0, 16
LANE_IDX, LANE_POS, LANE_GATE = 0, 4, 8


def _sigmoid(x):
    return 1.0 / (1.0 + jnp.exp(-x))


def _rms(x, g):
    ms = jnp.mean(x * x, axis=-1, keepdims=True)
    return x * lax.rsqrt(ms + NORM_EPS) * g


def _dot(a, b):
    return jnp.dot(a, b, preferred_element_type=F32)


def _dot_nt(a, b):
    return lax.dot_general(a, b, (((1,), (1,)), ((), ())), preferred_element_type=F32)


def _split_dot(v, m_bf16, terms):
    out = None
    for _ in range(terms):
        hi = v.astype(BF16)
        part = _dot(hi, m_bf16)
        out = part if out is None else out + part
        v = v - hi.astype(F32)
    return out


def _store_rows(ref, x):
    n = x.shape[0]
    for sb in range(ROW_SUB):
        ref[pl.ds(sb, n, stride=ROW_SUB), :] = x[:, sb * LANES:(sb + 1) * LANES]


def _load_rows(ref, n, lead=()):
    return jnp.concatenate(
        [ref[lead + (pl.ds(sb, n, stride=ROW_SUB), slice(None))] for sb in range(ROW_SUB)], axis=1)


def _params(sem, vmem=VMEM_LIMIT):
    return pltpu.CompilerParams(dimension_semantics=sem, vmem_limit_bytes=vmem)


def _inproj_kernel(x_ref, g_ref, wm_ref, ws_ref, main_ref, small_ref, u_ref):
    @pl.when(pl.program_id(1) == 0)
    def _():
        u = _rms(x_ref[...], g_ref[...]).astype(BF16)
        u_ref[...] = u
        small_ref[...] = _dot(u, ws_ref[...])

    main_ref[...] = _dot(u_ref[...], wm_ref[...]).astype(BF16)


def _inproj(x2, g, w_main, w_small, tm, tn):
    n, d = x2.shape
    return pl.pallas_call(
        _inproj_kernel,
        grid=(n // tm, MAIN_COLS // tn),
        in_specs=[
            pl.BlockSpec((tm, d), lambda i, j: (i, 0)),
            pl.BlockSpec((1, d), lambda i, j: (0, 0)),
            pl.BlockSpec((d, tn), lambda i, j: (0, j)),
            pl.BlockSpec((d, LANES), lambda i, j: (0, 0)),
        ],
        out_specs=[
            pl.BlockSpec((tm, tn), lambda i, j: (i, j)),
            pl.BlockSpec((tm, LANES), lambda i, j: (i, 0)),
        ],
        out_shape=[jax.ShapeDtypeStruct((n, MAIN_COLS), BF16), jax.ShapeDtypeStruct((n, LANES), F32)],
        scratch_shapes=[pltpu.VMEM((tm, d), BF16)],
        compiler_params=_params(("arbitrary", "arbitrary")),
        name="inproj",
    )(x2, g, w_main, w_small)


def _ssd_kernel(xbc_ref, z_ref, sm_ref, cw_ref, cb_ref, bias_ref, alog_ref, dsk_ref, ng_ref, e_ref,
                y_ref, ct_ref, state, buf, ccarry):
    L = SSD_CHUNK
    inner = SSD_HEADS * SSD_HEAD_DIM
    gw = inner // SSD_GROUPS
    hpg = SSD_HEADS // SSD_GROUPS

    @pl.when(pl.program_id(1) == 0)
    def _():
        state[...] = jnp.zeros_like(state)
        buf[0:CONV_HALO, :] = jnp.zeros((CONV_HALO, buf.shape[1]), F32)
        ccarry[...] = jnp.zeros_like(ccarry)

    xbc = xbc_ref[0].astype(F32)
    buf[CONV_HALO:CONV_HALO + L, :] = xbc
    acc = cb_ref[...] + cw_ref[SSD_CONV - 1:SSD_CONV, :] * xbc
    for j in range(SSD_CONV - 1):
        acc = acc + cw_ref[j:j + 1, :] * buf[pl.ds(CONV_HALO - (SSD_CONV - 1) + j, L), :]
    buf[0:CONV_HALO, :] = xbc[L - CONV_HALO:L, :]
    xc = acc * _sigmoid(acc)
    xs = xc[:, :inner]
    bm = xc[:, inner:inner + SSD_GROUPS * SSD_STATE]
    cm = xc[:, inner + SSD_GROUPS * SSD_STATE:]

    sm = sm_ref[0] + bias_ref[...]
    t = jnp.log1p(jnp.exp(-jnp.abs(sm)))
    dt = jnp.maximum(sm, 0.0) + t
    logf = jnp.minimum(sm, 0.0) - t
    lane = lax.broadcasted_iota(jnp.int32, (L, LANES), 1)
    is_dt = lane < LANE_F
    a = -jnp.exp(alog_ref[...]) * dt
    comb = jnp.where(is_dt, a, jnp.where(lane < LANE_F + FOX_HEADS, logf, 0.0))
    row = lax.broadcasted_iota(jnp.int32, (L, L), 0)
    col = lax.broadcasted_iota(jnp.int32, (L, L), 1)
    causal = row >= col
    tril = jnp.where(causal, 1.0, 0.0).astype(BF16)
    cum = _split_dot_left(tril, comb)
    cglob = cum + ccarry[...]
    ccarry[...] = cglob[L - 1:L, :]
    ct_ref[0] = (cglob * LOG2E).T

    acum = jnp.where(is_dt, cum, 0.0)
    act = acum.T
    alast = acum[L - 1:L, :]
    e_mat = e_ref[...]
    dt_e = _split_dot(jnp.where(is_dt, dt, 0.0), e_mat, 2)
    ds_e = _split_dot(jnp.where(is_dt, jnp.exp(alast - acum), 0.0), e_mat, 2)
    ea_e = _split_dot(jnp.where(is_dt, jnp.exp(acum), 0.0), e_mat, 2)
    cd_e = _split_dot(jnp.broadcast_to(jnp.where(is_dt[0:1], jnp.exp(alast), 0.0), (8, LANES)), e_mat, 2)[0:1]

    xdt = xs * dt_e
    xdt_b = xdt.astype(BF16)
    xds_b = (xdt * ds_e).astype(BF16)
    glane = lax.broadcasted_iota(jnp.int32, (L, gw), 1)
    ys = []
    for g in range(SSD_GROUPS):
        gs = slice(g * gw, (g + 1) * gw)
        bg = bm[:, g * SSD_STATE:(g + 1) * SSD_STATE]
        cg = cm[:, g * SSD_STATE:(g + 1) * SSD_STATE].astype(BF16)
        cb = _dot_nt(cg, bg.astype(BF16))
        xg = xdt_b[:, gs]
        ydiag = jnp.zeros((L, gw), F32)
        for e in range(hpg):
            h = g * hpg + e
            diff = acum[:, h:h + 1] - act[h:h + 1, :]
            dm = jnp.exp(jnp.where(causal, diff, -jnp.inf))
            yfull = _dot((cb * dm).astype(BF16), xg)
            ydiag = jnp.where((glane >= e * SSD_HEAD_DIM) & (glane < (e + 1) * SSD_HEAD_DIM), yfull, ydiag)
        stg = state[:, gs]
        yoff = _dot(cg, stg.astype(BF16)) * ea_e[:, gs]
        state[:, gs] = stg * cd_e[:, gs] + _dot(bg.T.astype(BF16), xds_b[:, gs])
        ys.append(ydiag + yoff)
    y = jnp.concatenate(ys, axis=1) + dsk_ref[...] * xs

    z = z_ref[0].astype(F32)
    gated = y * (z * _sigmoid(z))
    outs = []
    for g in range(SSD_GROUPS):
        seg = gated[:, g * gw:(g + 1) * gw]
        ms = jnp.mean(seg * seg, axis=-1, keepdims=True)
        outs.append(seg * lax.rsqrt(ms + NORM_EPS))
    y_ref[0] = (jnp.concatenate(outs, axis=1) * ng_ref[...]).astype(BF16)


def _split_dot_left(m_bf16, v):
    out = None
    for _ in range(3):
        hi = v.astype(BF16)
        part = _dot(m_bf16, hi)
        out = part if out is None else out + part
        v = v - hi.astype(F32)
    return out


def _ssd(main3, small3, conv_w, conv_b, bias_row, alog_row, dskip_row, normg_row, e_mat):
    b, s, _ = main3.shape
    L = SSD_CHUNK
    inner = SSD_HEADS * SSD_HEAD_DIM
    cc = conv_w.shape[1]
    const = lambda shape: pl.BlockSpec(shape, lambda i, c: (0,) * len(shape))
    return pl.pallas_call(
        _ssd_kernel,
        grid=(b, s // L),
        in_specs=[
            pl.BlockSpec((1, L, cc), lambda i, c: (i, c, COL_XBC // cc)),
            pl.BlockSpec((1, L, inner), lambda i, c: (i, c, COL_Z // inner)),
            pl.BlockSpec((1, L, LANES), lambda i, c: (i, c, 0)),
            const((SSD_CONV, cc)), const((1, cc)), const((1, LANES)), const((1, LANES)),
            const((1, inner)), const((1, inner)), const((LANES, inner)),
        ],
        out_specs=[
            pl.BlockSpec((1, L, inner), lambda i, c: (i, c, 0)),
            pl.BlockSpec((1, LANES, L), lambda i, c: (i, 0, c)),
        ],
        out_shape=[
            jax.ShapeDtypeStruct((b, s, inner), BF16),
            jax.ShapeDtypeStruct((b, LANES, s), F32),
        ],
        scratch_shapes=[
            pltpu.VMEM((SSD_STATE, inner), F32),
            pltpu.VMEM((L + CONV_HALO, cc), F32),
            pltpu.VMEM((1, LANES), F32),
        ],
        compiler_params=_params(("arbitrary", "arbitrary")),
        name="ssd",
    )(main3, main3, small3, conv_w, conv_b, bias_row, alog_row, dskip_row, normg_row, e_mat)


def _fox_kernel(q_ref, k_ref, v_ref, ck_ref, o_ref, m_sc, l_sc, acc_sc, *, tq, tk):
    h = pl.program_id(1)
    i = pl.program_id(2)
    m_sc[...] = jnp.full(m_sc.shape, NEG_BIG, F32)
    l_sc[...] = jnp.zeros_like(l_sc)
    acc_sc[...] = jnp.zeros_like(acc_sc)
    reps = tk // LANES

    def step(j, masked, r0):
        rows = slice(r0, tq)
        ks = pl.multiple_of(j * tk, tk)
        k = k_ref[0, pl.ds(ks, tk), :]
        v = v_ref[0, pl.ds(ks, tk), :]
        s = _dot_nt(q_ref[0, rows, :], k) - ck_ref[0, pl.ds(h, 1), pl.ds(ks, tk)]
        if masked:
            row = i * tq + r0 + lax.broadcasted_iota(jnp.int32, (tq - r0, tk), 0)
            col = j * tk + lax.broadcasted_iota(jnp.int32, (tq - r0, tk), 1)
            s = jnp.where(row >= col, s, -jnp.inf)
        m_prev = m_sc[rows, :]
        m_new = jnp.maximum(m_prev, jnp.max(s, axis=-1, keepdims=True))
        alpha = jnp.exp2(m_prev - m_new)
        p = jnp.exp2(s - jnp.concatenate([m_new] * reps, axis=1))
        l_sc[rows, :] = alpha * l_sc[rows, :] + jnp.sum(p, axis=-1, keepdims=True)
        acc_sc[rows, :] = alpha * acc_sc[rows, :] + _dot(p.astype(BF16), v)
        m_sc[rows, :] = m_new

    def body(j, carry):
        step(j, False, 0)
        return carry

    if tk >= tq:
        n_full, n_diag = (i * tq) // tk, 1
    else:
        n_full, n_diag = i * (tq // tk), tq // tk
    lax.fori_loop(0, n_full, body, 0)
    for jd in range(n_diag):
        step(n_full + jd, True, jd * tk if tk < tq else 0)
    o_ref[0] = (acc_sc[...] / l_sc[...]).astype(BF16)


def _fox(main3, ct3, tq, tk):
    b, s, _ = main3.shape
    dh = FOX_HEAD_DIM
    assert dh == LANES and (tk % tq == 0 or tq % tk == 0)
    return pl.pallas_call(
        functools.partial(_fox_kernel, tq=tq, tk=tk),
        grid=(b, FOX_HEADS, s // tq),
        in_specs=[
            pl.BlockSpec((1, tq, dh), lambda bi, h, i: (bi, i, COL_Q // dh + h)),
            pl.BlockSpec((1, s, dh), lambda bi, h, i: (bi, 0, COL_K // dh + h)),
            pl.BlockSpec((1, s, dh), lambda bi, h, i: (bi, 0, COL_V // dh + h)),
            pl.BlockSpec((1, FOX_HEADS, s), lambda bi, h, i: (bi, LANE_F // FOX_HEADS, 0)),
        ],
        out_specs=pl.BlockSpec((1, tq, dh), lambda bi, h, i: (bi, i, h)),
        out_shape=jax.ShapeDtypeStruct((b, s, FOX_HEADS * dh), BF16),
        scratch_shapes=[pltpu.VMEM((tq, LANES), F32), pltpu.VMEM((tq, LANES), F32), pltpu.VMEM((tq, dh), F32)],
        compiler_params=_params(("arbitrary", "arbitrary", "arbitrary")),
        name="fox",
    )(main3, main3, main3, ct3)


def _merge_kernel(ys_ref, yf_ref, gt_ref, bg_ref, x_ref, ws_ref, wf_ref, wo_ref, h_ref):
    d = x_ref.shape[1]
    g = _sigmoid(gt_ref[...].astype(F32) + bg_ref[...])
    merged = g[:, :d] * _dot(ys_ref[...], ws_ref[...]) + g[:, d:] * _dot(yf_ref[...], wf_ref[...])
    h_ref[...] = x_ref[...] + _dot(merged.astype(BF16), wo_ref[...])


def _merge(y_ssd, y_fox, main2, b_gates, x2, w_s, w_f, w_o, tm):
    n, d = x2.shape
    const = lambda shape: pl.BlockSpec(shape, lambda i: (0,) * len(shape))
    return pl.pallas_call(
        _merge_kernel,
        grid=(n // tm,),
        in_specs=[
            pl.BlockSpec((tm, d), lambda i: (i, 0)),
            pl.BlockSpec((tm, d), lambda i: (i, 0)),
            pl.BlockSpec((tm, 2 * d), lambda i: (i, COL_GATE // (2 * d))),
            const((1, 2 * d)),
            pl.BlockSpec((tm, d), lambda i: (i, 0)),
            const((d, d)), const((d, d)), const((d, d)),
        ],
        out_specs=pl.BlockSpec((tm, d), lambda i: (i, 0)),
        out_shape=jax.ShapeDtypeStruct((n, d), F32),
        compiler_params=_params(("arbitrary",)),
        name="merge",
    )(y_ssd, y_fox, main2, b_gates, x2, w_s, w_f, w_o)


def _memkv_kernel(mem_ref, g_ref, w_ref, kv_ref):
    kv_ref[0] = _dot(_rms(mem_ref[0], g_ref[...]).astype(BF16), w_ref[...]).astype(BF16)


def _memkv(mem, g, w_kv):
    b, m, d = mem.shape
    return pl.pallas_call(
        _memkv_kernel,
        grid=(b,),
        in_specs=[
            pl.BlockSpec((1, m, d), lambda i: (i, 0, 0)),
            pl.BlockSpec((1, d), lambda i: (0, 0)),
            pl.BlockSpec((d, 2 * d), lambda i: (0, 0)),
        ],
        out_specs=pl.BlockSpec((1, m, 2 * d), lambda i: (i, 0, 0)),
        out_shape=jax.ShapeDtypeStruct((b, m, 2 * d), BF16),
        compiler_params=_params(("arbitrary",)),
        name="memkv",
    )(mem, g, w_kv)


def _cross_kernel(h_ref, kv_ref, gc_ref, wq_ref, wo_ref, gf_ref, wr_ref, br_ref,
                  h2_ref, u3_ref, route_ref, cnt_ref, base):
    tm, d = h_ref.shape
    dh = d // X_HEADS

    @pl.when(pl.program_id(0) == 0)
    def _():
        base[...] = jnp.zeros_like(base)

    h1 = h_ref[...]
    q = _dot(_rms(h1, gc_ref[...]).astype(BF16), wq_ref[...]).astype(BF16)
    outs = []
    for hd in range(X_HEADS):
        kh = kv_ref[0, :, hd * dh:(hd + 1) * dh]
        vh = kv_ref[0, :, d + hd * dh:d + (hd + 1) * dh]
        s = _dot_nt(q[:, hd * dh:(hd + 1) * dh], kh) * (dh ** -0.5)
        p = jnp.exp(s - jnp.max(s, axis=-1, keepdims=True))
        outs.append(_dot(p.astype(BF16), vh) / jnp.sum(p, axis=-1, keepdims=True))
    h2 = h1 + _dot(jnp.concatenate(outs, axis=1).astype(BF16), wo_ref[...])
    h2_ref[...] = h2
    u3 = _rms(h2, gf_ref[...])
    _store_rows(u3_ref, u3)

    u_hi = u3.astype(BF16)
    u_lo = (u3 - u_hi.astype(F32)).astype(BF16)
    w_r = wr_ref[...]
    w_hi = w_r.astype(BF16)
    w_lo = (w_r - w_hi.astype(F32)).astype(BF16)
    logits = _dot(u_hi, w_hi) + _dot(u_hi, w_lo) + _dot(u_lo, w_hi) + br_ref[...]
    lane = lax.broadcasted_iota(jnp.int32, (tm, LANES), 1)
    lane_f = lane.astype(F32)
    vals, idxs = [], []
    work = logits
    for _ in range(TOP_K):
        m = jnp.max(work, axis=-1, keepdims=True)
        ix = jnp.min(jnp.where(work == m, lane_f, float(LANES)), axis=-1, keepdims=True)
        vals.append(m)
        idxs.append(ix)
        work = jnp.where(lane_f == ix, -jnp.inf, work)
    exps = [jnp.exp(v - vals[0]) for v in vals]
    den = exps[0] + exps[1] + exps[2] + exps[3]

    onehot = jnp.zeros((tm, LANES), F32)
    for ix in idxs:
        onehot = onehot + jnp.where(lane_f == ix, 1.0, 0.0)
    row = lax.broadcasted_iota(jnp.int32, (tm, tm), 0)
    col = lax.broadcasted_iota(jnp.int32, (tm, tm), 1)
    strict = jnp.where(row > col, 1.0, 0.0).astype(BF16)
    rank = _dot(strict, onehot.astype(BF16)) + base[...]
    route = jnp.zeros((tm, LANES), F32)
    for r in range(TOP_K):
        pos = jnp.sum(jnp.where(lane_f == idxs[r], rank, 0.0), axis=-1, keepdims=True)
        route = jnp.where(lane == LANE_IDX + r, idxs[r], route)
        route = jnp.where(lane == LANE_POS + r, pos, route)
        route = jnp.where(lane == LANE_GATE + r, exps[r] / den, route)
    route_ref[...] = route
    new_base = base[...] + jnp.sum(onehot, axis=0, keepdims=True)
    base[...] = new_base
    cnt_ref[...] = new_base


def _cross(h1, kv, g_cross, w_q, w_o, g_ffn, w_r, b_r, tm, seq):
    n, d = h1.shape
    m = kv.shape[1]
    per_batch = seq // tm
    const = lambda shape: pl.BlockSpec(shape, lambda i: (0,) * len(shape))
    return pl.pallas_call(
        _cross_kernel,
        grid=(n // tm,),
        in_specs=[
            pl.BlockSpec((tm, d), lambda i: (i, 0)),
            pl.BlockSpec((1, m, 2 * d), lambda i: (i // per_batch, 0, 0)),
            const((1, d)), const((d, d)), const((d, d)), const((1, d)),
            const((d, LANES)), const((1, LANES)),
        ],
        out_specs=[
            pl.BlockSpec((tm, d), lambda i: (i, 0)),
            pl.BlockSpec((tm * ROW_SUB, LANES), lambda i: (i, 0)),
            pl.BlockSpec((tm, LANES), lambda i: (i, 0)),
            pl.BlockSpec((1, LANES), lambda i: (0, 0)),
        ],
        out_shape=[
            jax.ShapeDtypeStruct((n, d), F32),
            jax.ShapeDtypeStruct((n * ROW_SUB, LANES), F32),
            jax.ShapeDtypeStruct((n, LANES), F32),
            jax.ShapeDtypeStruct((1, LANES), F32),
        ],
        scratch_shapes=[pltpu.VMEM((1, LANES), F32)],
        compiler_params=_params(("arbitrary",)),
        name="cross_router",
    )(h1, kv, g_cross, w_q, w_o, g_ffn, w_r, b_r)


def _dispatch_kernel(zb_ref, dest_ref, u_ref, xr_ref, zbuf, sem, zsem, *, td):
    i = pl.program_id(0)

    @pl.when(i == 0)
    def _():
        zbuf[...] = jnp.zeros_like(zbuf)

        def zero_copy(e):
            blk = EXPERT_ROWS * ROW_SUB
            return pltpu.make_async_copy(zbuf, xr_ref.at[pl.ds(pl.multiple_of(zb_ref[e] * blk, blk), blk)], zsem)

        for e in range(2 * N_EXPERTS):
            @pl.when(zb_ref[e] >= 0)
            def _():
                zero_copy(e).start()
        for e in range(2 * N_EXPERTS):
            @pl.when(zb_ref[e] >= 0)
            def _():
                zero_copy(e).wait()

    def issue(t, carry):
        for k in range(TOP_K):
            pltpu.make_async_copy(_row_tile(u_ref, t), _row_tile(xr_ref, dest_ref[t * TOP_K + k]),
                                  sem).start(priority=k % 2)
        return carry

    lax.fori_loop(0, td, issue, 0, unroll=4)
    for k in range(TOP_K):
        pltpu.make_async_copy(u_ref, xr_ref.at[pl.ds(0, td * ROW_SUB)], sem).wait()


def _row_tile(ref, r, lead=()):
    return ref.at[lead + (pl.ds(pl.multiple_of(r * ROW_SUB, ROW_SUB), ROW_SUB),)]


def _dispatch(zero_blk, dest_flat, u3, n_rows, td):
    n = u3.shape[0] // ROW_SUB
    return pl.pallas_call(
        functools.partial(_dispatch_kernel, td=td),
        grid_spec=pltpu.PrefetchScalarGridSpec(
            num_scalar_prefetch=1,
            grid=(n // td,),
            in_specs=[
                pl.BlockSpec((td * TOP_K,), lambda i, zb: (i,), memory_space=pltpu.SMEM),
                pl.BlockSpec((td * ROW_SUB, LANES), lambda i, zb: (i, 0)),
            ],
            out_specs=pl.BlockSpec(memory_space=pl.ANY),
            scratch_shapes=[
                pltpu.VMEM((EXPERT_ROWS * ROW_SUB, LANES), F32),
                pltpu.SemaphoreType.DMA(()),
                pltpu.SemaphoreType.DMA(()),
            ],
        ),
        out_shape=jax.ShapeDtypeStruct((n_rows * ROW_SUB, LANES), F32),
        compiler_params=_params(("arbitrary",)),
        name="dispatch",
    )(zero_blk, dest_flat, u3)


def _expert_kernel(be_ref, nu_ref, x_ref, w1_ref, b1_ref, w2_ref, b2_ref, y_ref, w1b, w2b):
    f = w2_ref.shape[1]
    i = pl.program_id(0)
    used = i < nu_ref[0]

    @pl.when(jnp.logical_and(used, jnp.logical_or(i == 0, be_ref[i] != be_ref[jnp.maximum(i - 1, 0)])))
    def _():
        w1b[...] = w1_ref[0].astype(BF16)
        w2b[...] = w2_ref[0].astype(BF16)

    @pl.when(used)
    def _():
        hgl = _dot(_load_rows(x_ref, EXPERT_ROWS).astype(BF16), w1b[...]) + b1_ref[0]
        glu = jnp.minimum(hgl[:, :f], SWIGLU_LIMIT)
        lin = jnp.clip(hgl[:, f:], -SWIGLU_LIMIT, SWIGLU_LIMIT)
        act = glu * _sigmoid(SWIGLU_ALPHA * glu) * (lin + 1.0)
        _store_rows(y_ref, _dot(act.astype(BF16), w2b[...]) + b2_ref[0])

    @pl.when(jnp.logical_not(used))
    def _():
        y_ref[...] = jnp.zeros_like(y_ref)


def _experts(blk_e, n_used, x_rows, w1, b1, w2, b2):
    f, d = w2.shape[1], w2.shape[2]
    blk = EXPERT_ROWS * ROW_SUB
    nb = x_rows.shape[0] // blk
    row_blk = lambda i, be, nu: (jnp.minimum(i, nu[0] - 1), 0)
    return pl.pallas_call(
        _expert_kernel,
        grid_spec=pltpu.PrefetchScalarGridSpec(
            num_scalar_prefetch=2,
            grid=(nb,),
            in_specs=[
                pl.BlockSpec((blk, LANES), row_blk),
                pl.BlockSpec((1, d, 2 * f), lambda i, be, nu: (be[i], 0, 0)),
                pl.BlockSpec((1, 1, 2 * f), lambda i, be, nu: (be[i], 0, 0)),
                pl.BlockSpec((1, f, d), lambda i, be, nu: (be[i], 0, 0)),
                pl.BlockSpec((1, 1, d), lambda i, be, nu: (be[i], 0, 0)),
            ],
            out_specs=pl.BlockSpec((blk, LANES), lambda i, be, nu: (i, 0)),
            scratch_shapes=[pltpu.VMEM((d, 2 * f), BF16), pltpu.VMEM((f, d), BF16)],
        ),
        out_shape=jax.ShapeDtypeStruct(x_rows.shape, F32),
        compiler_params=_params(("arbitrary",), EXPERT_VMEM_LIMIT),
        name="experts",
    )(blk_e, n_used, x_rows, w1, b1, w2, b2)


def _combine_kernel(dest_ref, y_ref, route_ref, h_ref, g_ref, o_ref, ybuf, sem, *, tc):
    def issue(t, carry):
        for k in range(TOP_K):
            pltpu.make_async_copy(_row_tile(y_ref, dest_ref[t * TOP_K + k]), _row_tile(ybuf, t, (k,)),
                                  sem).start(priority=k % 2)
        return carry

    lax.fori_loop(0, tc, issue, 0, unroll=4)
    for k in range(TOP_K):
        pltpu.make_async_copy(y_ref.at[pl.ds(0, tc * ROW_SUB)], ybuf.at[k], sem).wait()

    route = route_ref[...]
    lane = lax.broadcasted_iota(jnp.int32, route.shape, 1)
    acc = h_ref[...]
    for k in range(TOP_K):
        gate = jnp.sum(jnp.where(lane == LANE_GATE + k, route, 0.0), axis=-1, keepdims=True)
        acc = acc + gate * _load_rows(ybuf, tc, (k,))
    o_ref[...] = _rms(acc, g_ref[...])


def _combine(dest_flat, y_rows, route, h2, g_final, tc):
    n, d = h2.shape
    return pl.pallas_call(
        functools.partial(_combine_kernel, tc=tc),
        grid=(n // tc,),
        in_specs=[
            pl.BlockSpec((tc * TOP_K,), lambda i: (i,), memory_space=pltpu.SMEM),
            pl.BlockSpec(memory_space=pl.ANY),
            pl.BlockSpec((tc, LANES), lambda i: (i, 0)),
            pl.BlockSpec((tc, d), lambda i: (i, 0)),
            pl.BlockSpec((1, d), lambda i: (0, 0)),
        ],
        out_specs=pl.BlockSpec((tc, d), lambda i: (i, 0)),
        out_shape=jax.ShapeDtypeStruct((n, d), F32),
        scratch_shapes=[pltpu.VMEM((TOP_K, tc * ROW_SUB, LANES), F32), pltpu.SemaphoreType.DMA(())],
        compiler_params=_params(("arbitrary",)),
        name="combine",
    )(dest_flat, y_rows, route, h2, g_final)


def _pad_lanes(v, offset=0, fill=0.0):
    out = jnp.full((1, LANES), fill, F32)
    return lax.dynamic_update_slice(out, v.reshape(1, -1).astype(F32), (0, offset))


def _layer(h, mem, norm_mix_g, w_in, conv_w, conv_b, dt_bias, a_log, d_skip, ssd_norm_g,
           w_ssd_branch, b_forget, w_fox_branch, b_gates, w_out, norm_cross_g, norm_mem_g,
           w_cross_q, w_cross_kv, w_cross_o, norm_ffn_g, w_router, b_router,
           w_expert_in, b_expert_in, w_expert_out, b_expert_out, final_g):
    b, s, d = h.shape
    n = b * s
    inner = SSD_HEADS * SSD_HEAD_DIM
    cc = inner + 2 * SSD_GROUPS * SSD_STATE
    tm = min(512, s)

    cuts = [inner, cc, SSD_HEADS, d, d, d, FOX_HEADS, 2 * d]
    offs = [0]
    for c in cuts:
        offs.append(offs[-1] + c)
    wz, wxbc, wdt, wq, wk, wv, wf, wg = [w_in[:, offs[i]:offs[i + 1]] for i in range(8)]
    w_main = jnp.concatenate([wxbc, wz, wq * (LOG2E * FOX_HEAD_DIM ** -0.5), wk, wv, wg], axis=1).astype(BF16)
    w_small = jnp.concatenate(
        [wdt, wf, jnp.zeros((d, LANES - SSD_HEADS - FOX_HEADS), F32)], axis=1).astype(BF16)

    x2 = h.reshape(n, d)
    main, small = _inproj(x2, norm_mix_g.reshape(1, d), w_main, w_small, min(1024, n), 1024)
    main3 = main.reshape(b, s, MAIN_COLS)

    bias_row = _pad_lanes(jnp.concatenate([dt_bias, b_forget]))
    alog_row = _pad_lanes(a_log)
    dskip_row = jnp.repeat(d_skip, SSD_HEAD_DIM).reshape(1, inner)
    e_mat = (jnp.arange(LANES)[:, None] == (jnp.arange(inner)[None, :] // SSD_HEAD_DIM)).astype(BF16)
    y_ssd, ct3 = _ssd(main3, small.reshape(b, s, LANES), conv_w, conv_b.reshape(1, cc), bias_row,
                      alog_row, dskip_row, ssd_norm_g.reshape(1, inner), e_mat)
    y_fox = _fox(main3, ct3, min(FOX_TQ, s), min(FOX_TK, s))

    h1 = _merge(y_ssd.reshape(n, inner), y_fox.reshape(n, d), main, b_gates.reshape(1, 2 * d), x2,
                w_ssd_branch.astype(BF16), w_fox_branch.astype(BF16), w_out.astype(BF16), tm)

    kv = _memkv(mem, norm_mem_g.reshape(1, d), w_cross_kv.astype(BF16))
    w_r = jnp.concatenate([w_router, jnp.zeros((d, LANES - N_EXPERTS), F32)], axis=1)
    b_r = _pad_lanes(b_router, fill=NEG_BIG)
    h2, u3, route, counts = _cross(h1, kv, norm_cross_g.reshape(1, d), w_cross_q.astype(BF16),
                                   w_cross_o.astype(BF16), norm_ffn_g.reshape(1, d), w_r, b_r, tm, s)

    idx = route[:, LANE_IDX:LANE_IDX + TOP_K].astype(jnp.int32)
    pos = route[:, LANE_POS:LANE_POS + TOP_K].astype(jnp.int32)
    cnt = counts[0, :N_EXPERTS].astype(jnp.int32)
    nblk = (cnt + EXPERT_ROWS - 1) // EXPERT_ROWS
    bend = jnp.cumsum(nblk)
    bstart = bend - nblk
    n_used = bend[-1:]
    dest = (bstart[idx] * EXPERT_ROWS + pos).reshape(n * TOP_K)
    nb = n * TOP_K // EXPERT_ROWS + N_EXPERTS
    blk = jnp.minimum(jnp.arange(nb, dtype=jnp.int32), n_used[0] - 1)
    blk_e = jnp.minimum(jnp.sum(blk[:, None] >= bend[None, :], axis=1), N_EXPERTS - 1).astype(jnp.int32)
    tail = n_used[0] + jnp.arange(N_EXPERTS, dtype=jnp.int32)
    zero_blk = jnp.concatenate([jnp.where(nblk > 0, bend - 1, -1),
                                jnp.where(tail < nb, tail, -1)]).astype(jnp.int32)

    td = min(256, n)
    x_rows = _dispatch(zero_blk, dest, u3, nb * EXPERT_ROWS, td)
    y_rows = _experts(blk_e, n_used.astype(jnp.int32), x_rows, w_expert_in,
                      b_expert_in.reshape(N_EXPERTS, 1, -1), w_expert_out,
                      b_expert_out.reshape(N_EXPERTS, 1, -1))
    out = _combine(dest, y_rows, route, h2, final_g.reshape(1, d), td)
    return out.reshape(b, s, d)


def kernel(x, mem, norm_mix_g, w_in, conv_w, conv_b, dt_bias, a_log, d_skip, ssd_norm_g, w_ssd_branch,
           b_forget, w_fox_branch, b_gates, w_out, norm_cross_g, norm_mem_g, w_cross_q, w_cross_kv,
           w_cross_o, norm_ffn_g, w_router, b_router, w_expert_in, b_expert_in, w_expert_out,
           b_expert_out, norm_final_g):
    assert x.shape[2] == FOX_HEADS * FOX_HEAD_DIM == SSD_HEADS * SSD_HEAD_DIM
    assert norm_mix_g.shape[0] == 1, "single-layer problem"
    l = 0
    return _layer(x, mem, norm_mix_g[l], w_in[l], conv_w[l], conv_b[l], dt_bias[l], a_log[l], d_skip[l],
                  ssd_norm_g[l], w_ssd_branch[l], b_forget[l], w_fox_branch[l], b_gates[l], w_out[l],
                  norm_cross_g[l], norm_mem_g[l], w_cross_q[l], w_cross_kv[l], w_cross_o[l],
                  norm_ffn_g[l], w_router[l], b_router[l], w_expert_in[l], b_expert_in[l],
                  w_expert_out[l], b_expert_out[l], norm_final_g)
```

```python
import functools

import jax
import jax.numpy as jnp
from jax import lax
from jax.experimental import pallas as pl
from jax.experimental.pallas import tpu as pltpu

F32 = jnp.float32
BF16 = jnp.bfloat16

NORM_EPS = 1e-6
SSD_HEAD_DIM = 64
SSD_HEADS = 16
SSD_GROUPS = 4
SSD_STATE = 128
SSD_CONV = 4
SSD_CHUNK = 128
FOX_HEADS = 8
FOX_HEAD_DIM = 128
X_HEADS = 4
N_EXPERTS = 32
TOP_K = 4
SWIGLU_ALPHA = 1.702
SWIGLU_LIMIT = 7.0

LANES = 128
CONV_HALO = 8
ROW_SUB = 8
EXPERT_ROWS = 512
ROW_MOVE_TOKENS = 1024
NEG_BIG = -1e30
LOG2E = 1.4426950408889634
FOX_TQ, FOX_TK = 1024, 512
VMEM_LIMIT = 56 * 1024 * 1024
EXPERT_VMEM_LIMIT = 60 * 1024 * 1024

COL_XBC, COL_Z, COL_Q, COL_K, COL_V, COL_GATE, MAIN_COLS = 0, 2048, 3072, 4096, 5120, 6144, 8192
LANE_DT, LANE_F = 0, 16
LANE_IDX, LANE_POS, LANE_GATE = 0, 4, 8


def _sigmoid(x):
    return 1.0 / (1.0 + jnp.exp(-x))


def _rms(x, g):
    ms = jnp.mean(x * x, axis=-1, keepdims=True)
    return x * lax.rsqrt(ms + NORM_EPS) * g


def _dot(a, b):
    return jnp.dot(a, b, preferred_element_type=F32)


def _dot_nt(a, b):
    return lax.dot_general(a, b, (((1,), (1,)), ((), ())), preferred_element_type=F32)


def _split_dot(v, m_bf16, terms):
    out = None
    for _ in range(terms):
        hi = v.astype(BF16)
        part = _dot(hi, m_bf16)
        out = part if out is None else out + part
        v = v - hi.astype(F32)
    return out


def _store_rows(ref, x):
    n = x.shape[0]
    for sb in range(ROW_SUB):
        ref[pl.ds(sb, n, stride=ROW_SUB), :] = x[:, sb * LANES:(sb + 1) * LANES]


def _load_rows(ref, n, lead=()):
    return jnp.concatenate(
        [ref[lead + (pl.ds(sb, n, stride=ROW_SUB), slice(None))] for sb in range(ROW_SUB)], axis=1)


def _params(sem, vmem=VMEM_LIMIT):
    return pltpu.CompilerParams(dimension_semantics=sem, vmem_limit_bytes=vmem)


def _inproj_kernel(x_ref, g_ref, wm_ref, ws_ref, main_ref, small_ref, u_ref):
    @pl.when(pl.program_id(1) == 0)
    def _():
        u = _rms(x_ref[...], g_ref[...]).astype(BF16)
        u_ref[...] = u
        small_ref[...] = _dot(u, ws_ref[...])

    main_ref[...] = _dot(u_ref[...], wm_ref[...]).astype(BF16)


def _inproj(x2, g, w_main, w_small, tm, tn):
    n, d = x2.shape
    return pl.pallas_call(
        _inproj_kernel,
        grid=(n // tm, MAIN_COLS // tn),
        in_specs=[
            pl.BlockSpec((tm, d), lambda i, j: (i, 0)),
            pl.BlockSpec((1, d), lambda i, j: (0, 0)),
            pl.BlockSpec((d, tn), lambda i, j: (0, j)),
            pl.BlockSpec((d, LANES), lambda i, j: (0, 0)),
        ],
        out_specs=[
            pl.BlockSpec((tm, tn), lambda i, j: (i, j)),
            pl.BlockSpec((tm, LANES), lambda i, j: (i, 0)),
        ],
        out_shape=[jax.ShapeDtypeStruct((n, MAIN_COLS), BF16), jax.ShapeDtypeStruct((n, LANES), F32)],
        scratch_shapes=[pltpu.VMEM((tm, d), BF16)],
        compiler_params=_params(("arbitrary", "arbitrary")),
        name="inproj",
    )(x2, g, w_main, w_small)


def _ssd_kernel(xbc_ref, z_ref, sm_ref, cw_ref, cb_ref, bias_ref, alog_ref, dsk_ref, ng_ref, e_ref,
                y_ref, ct_ref, state, buf, ccarry):
    L = SSD_CHUNK
    inner = SSD_HEADS * SSD_HEAD_DIM
    gw = inner // SSD_GROUPS
    hpg = SSD_HEADS // SSD_GROUPS

    @pl.when(pl.program_id(1) == 0)
    def _():
        state[...] = jnp.zeros_like(state)
        buf[0:CONV_HALO, :] = jnp.zeros((CONV_HALO, buf.shape[1]), F32)
        ccarry[...] = jnp.zeros_like(ccarry)

    xbc = xbc_ref[0].astype(F32)
    buf[CONV_HALO:CONV_HALO + L, :] = xbc
    acc = cb_ref[...] + cw_ref[SSD_CONV - 1:SSD_CONV, :] * xbc
    for j in range(SSD_CONV - 1):
        acc = acc + cw_ref[j:j + 1, :] * buf[pl.ds(CONV_HALO - (SSD_CONV - 1) + j, L), :]
    buf[0:CONV_HALO, :] = xbc[L - CONV_HALO:L, :]
    xc = acc * _sigmoid(acc)
    xs = xc[:, :inner]
    bm = xc[:, inner:inner + SSD_GROUPS * SSD_STATE]
    cm = xc[:, inner + SSD_GROUPS * SSD_STATE:]

    sm = sm_ref[0] + bias_ref[...]
    t = jnp.log1p(jnp.exp(-jnp.abs(sm)))
    dt = jnp.maximum(sm, 0.0) + t
    logf = jnp.minimum(sm, 0.0) - t
    lane = lax.broadcasted_iota(jnp.int32, (L, LANES), 1)
    is_dt = lane < LANE_F
    a = -jnp.exp(alog_ref[...]) * dt
    comb = jnp.where(is_dt, a, jnp.where(lane < LANE_F + FOX_HEADS, logf, 0.0))
    row = lax.broadcasted_iota(jnp.int32, (L, L), 0)
    col = lax.broadcasted_iota(jnp.int32, (L, L), 1)
    causal = row >= col
    tril = jnp.where(causal, 1.0, 0.0).astype(BF16)
    cum = _split_dot_left(tril, comb)
    cglob = cum + ccarry[...]
    ccarry[...] = cglob[L - 1:L, :]
    ct_ref[0] = (cglob * LOG2E).T

    acum = jnp.where(is_dt, cum, 0.0)
    act = acum.T
    alast = acum[L - 1:L, :]
    e_mat = e_ref[...]
    dt_e = _split_dot(jnp.where(is_dt, dt, 0.0), e_mat, 2)
    ds_e = _split_dot(jnp.where(is_dt, jnp.exp(alast - acum), 0.0), e_mat, 2)
    ea_e = _split_dot(jnp.where(is_dt, jnp.exp(acum), 0.0), e_mat, 2)
    cd_e = _split_dot(jnp.broadcast_to(jnp.where(is_dt[0:1], jnp.exp(alast), 0.0), (8, LANES)), e_mat, 2)[0:1]

    xdt = xs * dt_e
    xdt_b = xdt.astype(BF16)
    xds_b = (xdt * ds_e).astype(BF16)
    glane = lax.broadcasted_iota(jnp.int32, (L, gw), 1)
    ys = []
    for g in range(SSD_GROUPS):
        gs = slice(g * gw, (g + 1) * gw)
        bg = bm[:, g * SSD_STATE:(g + 1) * SSD_STATE]
        cg = cm[:, g * SSD_STATE:(g + 1) * SSD_STATE].astype(BF16)
        cb = _dot_nt(cg, bg.astype(BF16))
        xg = xdt_b[:, gs]
        ydiag = jnp.zeros((L, gw), F32)
        for e in range(hpg):
            h = g * hpg + e
            diff = acum[:, h:h + 1] - act[h:h + 1, :]
            dm = jnp.exp(jnp.where(causal, diff, -jnp.inf))
            yfull = _dot((cb * dm).astype(BF16), xg)
            ydiag = jnp.where((glane >= e * SSD_HEAD_DIM) & (glane < (e + 1) * SSD_HEAD_DIM), yfull, ydiag)
        stg = state[:, gs]
        yoff = _dot(cg, stg.astype(BF16)) * ea_e[:, gs]
        state[:, gs] = stg * cd_e[:, gs] + _dot(bg.T.astype(BF16), xds_b[:, gs])
        ys.append(ydiag + yoff)
    y = jnp.concatenate(ys, axis=1) + dsk_ref[...] * xs

    z = z_ref[0].astype(F32)
    gated = y * (z * _sigmoid(z))
    outs = []
    for g in range(SSD_GROUPS):
        seg = gated[:, g * gw:(g + 1) * gw]
        ms = jnp.mean(seg * seg, axis=-1, keepdims=True)
        outs.append(seg * lax.rsqrt(ms + NORM_EPS))
    y_ref[0] = (jnp.concatenate(outs, axis=1) * ng_ref[...]).astype(BF16)


def _split_dot_left(m_bf16, v):
    out = None
    for _ in range(3):
        hi = v.astype(BF16)
        part = _dot(m_bf16, hi)
        out = part if out is None else out + part
        v = v - hi.astype(F32)
    return out


def _ssd(main3, small3, conv_w, conv_b, bias_row, alog_row, dskip_row, normg_row, e_mat):
    b, s, _ = main3.shape
    L = SSD_CHUNK
    inner = SSD_HEADS * SSD_HEAD_DIM
    cc = conv_w.shape[1]
    const = lambda shape: pl.BlockSpec(shape, lambda i, c: (0,) * len(shape))
    return pl.pallas_call(
        _ssd_kernel,
        grid=(b, s // L),
        in_specs=[
            pl.BlockSpec((1, L, cc), lambda i, c: (i, c, COL_XBC // cc)),
            pl.BlockSpec((1, L, inner), lambda i, c: (i, c, COL_Z // inner)),
            pl.BlockSpec((1, L, LANES), lambda i, c: (i, c, 0)),
            const((SSD_CONV, cc)), const((1, cc)), const((1, LANES)), const((1, LANES)),
            const((1, inner)), const((1, inner)), const((LANES, inner)),
        ],
        out_specs=[
            pl.BlockSpec((1, L, inner), lambda i, c: (i, c, 0)),
            pl.BlockSpec((1, LANES, L), lambda i, c: (i, 0, c)),
        ],
        out_shape=[
            jax.ShapeDtypeStruct((b, s, inner), BF16),
            jax.ShapeDtypeStruct((b, LANES, s), F32),
        ],
        scratch_shapes=[
            pltpu.VMEM((SSD_STATE, inner), F32),
            pltpu.VMEM((L + CONV_HALO, cc), F32),
            pltpu.VMEM((1, LANES), F32),
        ],
        compiler_params=_params(("arbitrary", "arbitrary")),
        name="ssd",
    )(main3, main3, small3, conv_w, conv_b, bias_row, alog_row, dskip_row, normg_row, e_mat)


def _fox_kernel(q_ref, k_ref, v_ref, ck_ref, o_ref, m_sc, l_sc, acc_sc, *, tq, tk):
    h = pl.program_id(1)
    i = pl.program_id(2)
    m_sc[...] = jnp.full(m_sc.shape, NEG_BIG, F32)
    l_sc[...] = jnp.zeros_like(l_sc)
    acc_sc[...] = jnp.zeros_like(acc_sc)
    reps = tk // LANES

    def step(j, masked, r0):
        rows = slice(r0, tq)
        ks = pl.multiple_of(j * tk, tk)
        k = k_ref[0, pl.ds(ks, tk), :]
        v = v_ref[0, pl.ds(ks, tk), :]
        s = _dot_nt(q_ref[0, rows, :], k) - ck_ref[0, pl.ds(h, 1), pl.ds(ks, tk)]
        if masked:
            row = i * tq + r0 + lax.broadcasted_iota(jnp.int32, (tq - r0, tk), 0)
            col = j * tk + lax.broadcasted_iota(jnp.int32, (tq - r0, tk), 1)
            s = jnp.where(row >= col, s, -jnp.inf)
        m_prev = m_sc[rows, :]
        m_new = jnp.maximum(m_prev, jnp.max(s, axis=-1, keepdims=True))
        alpha = jnp.exp2(m_prev - m_new)
        p = jnp.exp2(s - jnp.concatenate([m_new] * reps, axis=1))
        l_sc[rows, :] = alpha * l_sc[rows, :] + jnp.sum(p, axis=-1, keepdims=True)
        acc_sc[rows, :] = alpha * acc_sc[rows, :] + _dot(p.astype(BF16), v)
        m_sc[rows, :] = m_new

    def body(j, carry):
        step(j, False, 0)
        return carry

    if tk >= tq:
        n_full, n_diag = (i * tq) // tk, 1
    else:
        n_full, n_diag = i * (tq // tk), tq // tk
    lax.fori_loop(0, n_full, body, 0)
    for jd in range(n_diag):
        step(n_full + jd, True, jd * tk if tk < tq else 0)
    o_ref[0] = (acc_sc[...] / l_sc[...]).astype(BF16)


def _fox(main3, ct3, tq, tk):
    b, s, _ = main3.shape
    dh = FOX_HEAD_DIM
    assert dh == LANES and (tk % tq == 0 or tq % tk == 0)
    return pl.pallas_call(
        functools.partial(_fox_kernel, tq=tq, tk=tk),
        grid=(b, FOX_HEADS, s // tq),
        in_specs=[
            pl.BlockSpec((1, tq, dh), lambda bi, h, i: (bi, i, COL_Q // dh + h)),
            pl.BlockSpec((1, s, dh), lambda bi, h, i: (bi, 0, COL_K // dh + h)),
            pl.BlockSpec((1, s, dh), lambda bi, h, i: (bi, 0, COL_V // dh + h)),
            pl.BlockSpec((1, FOX_HEADS, s), lambda bi, h, i: (bi, LANE_F // FOX_HEADS, 0)),
        ],
        out_specs=pl.BlockSpec((1, tq, dh), lambda bi, h, i: (bi, i, h)),
        out_shape=jax.ShapeDtypeStruct((b, s, FOX_HEADS * dh), BF16),
        scratch_shapes=[pltpu.VMEM((tq, LANES), F32), pltpu.VMEM((tq, LANES), F32), pltpu.VMEM((tq, dh), F32)],
        compiler_params=_params(("arbitrary", "arbitrary", "arbitrary")),
        name="fox",
    )(main3, main3, main3, ct3)


def _merge_kernel(ys_ref, yf_ref, gt_ref, bg_ref, x_ref, ws_ref, wf_ref, wo_ref, h_ref):
    d = x_ref.shape[1]
    g = _sigmoid(gt_ref[...].astype(F32) + bg_ref[...])
    merged = g[:, :d] * _dot(ys_ref[...], ws_ref[...]) + g[:, d:] * _dot(yf_ref[...], wf_ref[...])
    h_ref[...] = x_ref[...] + _dot(merged.astype(BF16), wo_ref[...])


def _merge(y_ssd, y_fox, main2, b_gates, x2, w_s, w_f, w_o, tm):
    n, d = x2.shape
    const = lambda shape: pl.BlockSpec(shape, lambda i: (0,) * len(shape))
    return pl.pallas_call(
        _merge_kernel,
        grid=(n // tm,),
        in_specs=[
            pl.BlockSpec((tm, d), lambda i: (i, 0)),
            pl.BlockSpec((tm, d), lambda i: (i, 0)),
            pl.BlockSpec((tm, 2 * d), lambda i: (i, COL_GATE // (2 * d))),
            const((1, 2 * d)),
            pl.BlockSpec((tm, d), lambda i: (i, 0)),
            const((d, d)), const((d, d)), const((d, d)),
        ],
        out_specs=pl.BlockSpec((tm, d), lambda i: (i, 0)),
        out_shape=jax.ShapeDtypeStruct((n, d), F32),
        compiler_params=_params(("arbitrary",)),
        name="merge",
    )(y_ssd, y_fox, main2, b_gates, x2, w_s, w_f, w_o)


def _memkv_kernel(mem_ref, g_ref, w_ref, kv_ref):
    kv_ref[0] = _dot(_rms(mem_ref[0], g_ref[...]).astype(BF16), w_ref[...]).astype(BF16)


def _memkv(mem, g, w_kv):
    b, m, d = mem.shape
    return pl.pallas_call(
        _memkv_kernel,
        grid=(b,),
        in_specs=[
            pl.BlockSpec((1, m, d), lambda i: (i, 0, 0)),
            pl.BlockSpec((1, d), lambda i: (0, 0)),
            pl.BlockSpec((d, 2 * d), lambda i: (0, 0)),
        ],
        out_specs=pl.BlockSpec((1, m, 2 * d), lambda i: (i, 0, 0)),
        out_shape=jax.ShapeDtypeStruct((b, m, 2 * d), BF16),
        compiler_params=_params(("arbitrary",)),
        name="memkv",
    )(mem, g, w_kv)


def _cross_kernel(h_ref, kv_ref, gc_ref, wq_ref, wo_ref, gf_ref, wr_ref, br_ref,
                  h2_ref, u3_ref, route_ref, cnt_ref, base):
    tm, d = h_ref.shape
    dh = d // X_HEADS

    @pl.when(pl.program_id(0) == 0)
    def _():
        base[...] = jnp.zeros_like(base)

    h1 = h_ref[...]
    q = _dot(_rms(h1, gc_ref[...]).astype(BF16), wq_ref[...]).astype(BF16)
    outs = []
    for hd in range(X_HEADS):
        kh = kv_ref[0, :, hd * dh:(hd + 1) * dh]
        vh = kv_ref[0, :, d + hd * dh:d + (hd + 1) * dh]
        s = _dot_nt(q[:, hd * dh:(hd + 1) * dh], kh) * (dh ** -0.5)
        p = jnp.exp(s - jnp.max(s, axis=-1, keepdims=True))
        outs.append(_dot(p.astype(BF16), vh) / jnp.sum(p, axis=-1, keepdims=True))
    h2 = h1 + _dot(jnp.concatenate(outs, axis=1).astype(BF16), wo_ref[...])
    h2_ref[...] = h2
    u3 = _rms(h2, gf_ref[...])
    _store_rows(u3_ref, u3)

    u_hi = u3.astype(BF16)
    u_lo = (u3 - u_hi.astype(F32)).astype(BF16)
    w_r = wr_ref[...]
    w_hi = w_r.astype(BF16)
    w_lo = (w_r - w_hi.astype(F32)).astype(BF16)
    logits = _dot(u_hi, w_hi) + _dot(u_hi, w_lo) + _dot(u_lo, w_hi) + br_ref[...]
    lane = lax.broadcasted_iota(jnp.int32, (tm, LANES), 1)
    lane_f = lane.astype(F32)
    vals, idxs = [], []
    work = logits
    for _ in range(TOP_K):
        m = jnp.max(work, axis=-1, keepdims=True)
        ix = jnp.min(jnp.where(work == m, lane_f, float(LANES)), axis=-1, keepdims=True)
        vals.append(m)
        idxs.append(ix)
        work = jnp.where(lane_f == ix, -jnp.inf, work)
    exps = [jnp.exp(v - vals[0]) for v in vals]
    den = exps[0] + exps[1] + exps[2] + exps[3]

    onehot = jnp.zeros((tm, LANES), F32)
    for ix in idxs:
        onehot = onehot + jnp.where(lane_f == ix, 1.0, 0.0)
    row = lax.broadcasted_iota(jnp.int32, (tm, tm), 0)
    col = lax.broadcasted_iota(jnp.int32, (tm, tm), 1)
    strict = jnp.where(row > col, 1.0, 0.0).astype(BF16)
    rank = _dot(strict, onehot.astype(BF16)) + base[...]
    route = jnp.zeros((tm, LANES), F32)
    for r in range(TOP_K):
        pos = jnp.sum(jnp.where(lane_f == idxs[r], rank, 0.0), axis=-1, keepdims=True)
        route = jnp.where(lane == LANE_IDX + r, idxs[r], route)
        route = jnp.where(lane == LANE_POS + r, pos, route)
        route = jnp.where(lane == LANE_GATE + r, exps[r] / den, route)
    route_ref[...] = route
    new_base = base[...] + jnp.sum(onehot, axis=0, keepdims=True)
    base[...] = new_base
    cnt_ref[...] = new_base


def _cross(h1, kv, g_cross, w_q, w_o, g_ffn, w_r, b_r, tm, seq):
    n, d = h1.shape
    m = kv.shape[1]
    per_batch = seq // tm
    const = lambda shape: pl.BlockSpec(shape, lambda i: (0,) * len(shape))
    return pl.pallas_call(
        _cross_kernel,
        grid=(n // tm,),
        in_specs=[
            pl.BlockSpec((tm, d), lambda i: (i, 0)),
            pl.BlockSpec((1, m, 2 * d), lambda i: (i // per_batch, 0, 0)),
            const((1, d)), const((d, d)), const((d, d)), const((1, d)),
            const((d, LANES)), const((1, LANES)),
        ],
        out_specs=[
            pl.BlockSpec((tm, d), lambda i: (i, 0)),
            pl.BlockSpec((tm * ROW_SUB, LANES), lambda i: (i, 0)),
            pl.BlockSpec((tm, LANES), lambda i: (i, 0)),
            pl.BlockSpec((1, LANES), lambda i: (0, 0)),
        ],
        out_shape=[
            jax.ShapeDtypeStruct((n, d), F32),
            jax.ShapeDtypeStruct((n * ROW_SUB, LANES), F32),
            jax.ShapeDtypeStruct((n, LANES), F32),
            jax.ShapeDtypeStruct((1, LANES), F32),
        ],
        scratch_shapes=[pltpu.VMEM((1, LANES), F32)],
        compiler_params=_params(("arbitrary",)),
        name="cross_router",
    )(h1, kv, g_cross, w_q, w_o, g_ffn, w_r, b_r)


def _dispatch_kernel(zb_ref, dest_ref, u_ref, xr_ref, zbuf, sem, zsem, *, td):
    i = pl.program_id(0)

    @pl.when(i == 0)
    def _():
        zbuf[...] = jnp.zeros_like(zbuf)

        def zero_copy(e):
            blk = EXPERT_ROWS * ROW_SUB
            return pltpu.make_async_copy(zbuf, xr_ref.at[pl.ds(pl.multiple_of(zb_ref[e] * blk, blk), blk)], zsem)

        for e in range(2 * N_EXPERTS):
            @pl.when(zb_ref[e] >= 0)
            def _():
                zero_copy(e).start()
        for e in range(2 * N_EXPERTS):
            @pl.when(zb_ref[e] >= 0)
            def _():
                zero_copy(e).wait()

    def issue(t, carry):
        for k in range(TOP_K):
            pltpu.make_async_copy(_row_tile(u_ref, t), _row_tile(xr_ref, dest_ref[t * TOP_K + k]),
                                  sem).start(priority=k % 2)
        return carry

    lax.fori_loop(0, td, issue, 0, unroll=4)
    for k in range(TOP_K):
        pltpu.make_async_copy(u_ref, xr_ref.at[pl.ds(0, td * ROW_SUB)], sem).wait()


def _row_tile(ref, r, lead=()):
    return ref.at[lead + (pl.ds(pl.multiple_of(r * ROW_SUB, ROW_SUB), ROW_SUB),)]


def _dispatch(zero_blk, dest_flat, u3, n_rows, td):
    n = u3.shape[0] // ROW_SUB
    return pl.pallas_call(
        functools.partial(_dispatch_kernel, td=td),
        grid_spec=pltpu.PrefetchScalarGridSpec(
            num_scalar_prefetch=1,
            grid=(n // td,),
            in_specs=[
                pl.BlockSpec((td * TOP_K,), lambda i, zb: (i,), memory_space=pltpu.SMEM),
                pl.BlockSpec((td * ROW_SUB, LANES), lambda i, zb: (i, 0)),
            ],
            out_specs=pl.BlockSpec(memory_space=pl.ANY),
            scratch_shapes=[
                pltpu.VMEM((EXPERT_ROWS * ROW_SUB, LANES), F32),
                pltpu.SemaphoreType.DMA(()),
                pltpu.SemaphoreType.DMA(()),
            ],
        ),
        out_shape=jax.ShapeDtypeStruct((n_rows * ROW_SUB, LANES), F32),
        compiler_params=_params(("arbitrary",)),
        name="dispatch",
    )(zero_blk, dest_flat, u3)


def _expert_kernel(be_ref, nu_ref, x_ref, w1_ref, b1_ref, w2_ref, b2_ref, y_ref, w1b, w2b):
    f = w2_ref.shape[1]
    i = pl.program_id(0)
    used = i < nu_ref[0]

    @pl.when(jnp.logical_and(used, jnp.logical_or(i == 0, be_ref[i] != be_ref[jnp.maximum(i - 1, 0)])))
    def _():
        w1b[...] = w1_ref[0].astype(BF16)
        w2b[...] = w2_ref[0].astype(BF16)

    @pl.when(used)
    def _():
        hgl = _dot(_load_rows(x_ref, EXPERT_ROWS).astype(BF16), w1b[...]) + b1_ref[0]
        glu = jnp.minimum(hgl[:, :f], SWIGLU_LIMIT)
        lin = jnp.clip(hgl[:, f:], -SWIGLU_LIMIT, SWIGLU_LIMIT)
        act = glu * _sigmoid(SWIGLU_ALPHA * glu) * (lin + 1.0)
        _store_rows(y_ref, _dot(act.astype(BF16), w2b[...]) + b2_ref[0])

    @pl.when(jnp.logical_not(used))
    def _():
        y_ref[...] = jnp.zeros_like(y_ref)


def _experts(blk_e, n_used, x_rows, w1, b1, w2, b2):
    f, d = w2.shape[1], w2.shape[2]
    blk = EXPERT_ROWS * ROW_SUB
    nb = x_rows.shape[0] // blk
    row_blk = lambda i, be, nu: (jnp.minimum(i, nu[0] - 1), 0)
    return pl.pallas_call(
        _expert_kernel,
        grid_spec=pltpu.PrefetchScalarGridSpec(
            num_scalar_prefetch=2,
            grid=(nb,),
            in_specs=[
                pl.BlockSpec((blk, LANES), row_blk),
                pl.BlockSpec((1, d, 2 * f), lambda i, be, nu: (be[i], 0, 0)),
                pl.BlockSpec((1, 1, 2 * f), lambda i, be, nu: (be[i], 0, 0)),
                pl.BlockSpec((1, f, d), lambda i, be, nu: (be[i], 0, 0)),
                pl.BlockSpec((1, 1, d), lambda i, be, nu: (be[i], 0, 0)),
            ],
            out_specs=pl.BlockSpec((blk, LANES), lambda i, be, nu: (i, 0)),
            scratch_shapes=[pltpu.VMEM((d, 2 * f), BF16), pltpu.VMEM((f, d), BF16)],
        ),
        out_shape=jax.ShapeDtypeStruct(x_rows.shape, F32),
        compiler_params=_params(("arbitrary",), EXPERT_VMEM_LIMIT),
        name="experts",
    )(blk_e, n_used, x_rows, w1, b1, w2, b2)


def _combine_kernel(dest_ref, y_ref, route_ref, h_ref, g_ref, o_ref, ybuf, sem, *, tc):
    def issue(t, carry):
        for k in range(TOP_K):
            pltpu.make_async_copy(_row_tile(y_ref, dest_ref[t * TOP_K + k]), _row_tile(ybuf, t, (k,)),
                                  sem).start(priority=k % 2)
        return carry

    lax.fori_loop(0, tc, issue, 0, unroll=4)
    for k in range(TOP_K):
        pltpu.make_async_copy(y_ref.at[pl.ds(0, tc * ROW_SUB)], ybuf.at[k], sem).wait()

    route = route_ref[...]
    lane = lax.broadcasted_iota(jnp.int32, route.shape, 1)
    acc = h_ref[...]
    for k in range(TOP_K):
        gate = jnp.sum(jnp.where(lane == LANE_GATE + k, route, 0.0), axis=-1, keepdims=True)
        acc = acc + gate * _load_rows(ybuf, tc, (k,))
    o_ref[...] = _rms(acc, g_ref[...])


def _combine(dest_flat, y_rows, route, h2, g_final, tc):
    n, d = h2.shape
    return pl.pallas_call(
        functools.partial(_combine_kernel, tc=tc),
        grid=(n // tc,),
        in_specs=[
            pl.BlockSpec((tc * TOP_K,), lambda i: (i,), memory_space=pltpu.SMEM),
            pl.BlockSpec(memory_space=pl.ANY),
            pl.BlockSpec((tc, LANES), lambda i: (i, 0)),
            pl.BlockSpec((tc, d), lambda i: (i, 0)),
            pl.BlockSpec((1, d), lambda i: (0, 0)),
        ],
        out_specs=pl.BlockSpec((tc, d), lambda i: (i, 0)),
        out_shape=jax.ShapeDtypeStruct((n, d), F32),
        scratch_shapes=[pltpu.VMEM((TOP_K, tc * ROW_SUB, LANES), F32), pltpu.SemaphoreType.DMA(())],
        compiler_params=_params(("arbitrary",)),
        name="combine",
    )(dest_flat, y_rows, route, h2, g_final)


def _pad_lanes(v, offset=0, fill=0.0):
    out = jnp.full((1, LANES), fill, F32)
    return lax.dynamic_update_slice(out, v.reshape(1, -1).astype(F32), (0, offset))


def _layer(h, mem, norm_mix_g, w_in, conv_w, conv_b, dt_bias, a_log, d_skip, ssd_norm_g,
           w_ssd_branch, b_forget, w_fox_branch, b_gates, w_out, norm_cross_g, norm_mem_g,
           w_cross_q, w_cross_kv, w_cross_o, norm_ffn_g, w_router, b_router,
           w_expert_in, b_expert_in, w_expert_out, b_expert_out, final_g):
    b, s, d = h.shape
    n = b * s
    inner = SSD_HEADS * SSD_HEAD_DIM
    cc = inner + 2 * SSD_GROUPS * SSD_STATE
    tm = min(512, s)

    cuts = [inner, cc, SSD_HEADS, d, d, d, FOX_HEADS, 2 * d]
    offs = [0]
    for c in cuts:
        offs.append(offs[-1] + c)
    wz, wxbc, wdt, wq, wk, wv, wf, wg = [w_in[:, offs[i]:offs[i + 1]] for i in range(8)]
    w_main = jnp.concatenate([wxbc, wz, wq * (LOG2E * FOX_HEAD_DIM ** -0.5), wk, wv, wg], axis=1).astype(BF16)
    w_small = jnp.concatenate(
        [wdt, wf, jnp.zeros((d, LANES - SSD_HEADS - FOX_HEADS), F32)], axis=1).astype(BF16)

    x2 = h.reshape(n, d)
    main, small = _inproj(x2, norm_mix_g.reshape(1, d), w_main, w_small, min(1024, n), 2048)
    main3 = main.reshape(b, s, MAIN_COLS)

    bias_row = _pad_lanes(jnp.concatenate([dt_bias, b_forget]))
    alog_row = _pad_lanes(a_log)
    dskip_row = jnp.repeat(d_skip, SSD_HEAD_DIM).reshape(1, inner)
    e_mat = (jnp.arange(LANES)[:, None] == (jnp.arange(inner)[None, :] // SSD_HEAD_DIM)).astype(BF16)
    y_ssd, ct3 = _ssd(main3, small.reshape(b, s, LANES), conv_w, conv_b.reshape(1, cc), bias_row,
                      alog_row, dskip_row, ssd_norm_g.reshape(1, inner), e_mat)
    y_fox = _fox(main3, ct3, min(FOX_TQ, s), min(FOX_TK, s))

    h1 = _merge(y_ssd.reshape(n, inner), y_fox.reshape(n, d), main, b_gates.reshape(1, 2 * d), x2,
                w_ssd_branch.astype(BF16), w_fox_branch.astype(BF16), w_out.astype(BF16), tm)

    kv = _memkv(mem, norm_mem_g.reshape(1, d), w_cross_kv.astype(BF16))
    w_r = jnp.concatenate([w_router, jnp.zeros((d, LANES - N_EXPERTS), F32)], axis=1)
    b_r = _pad_lanes(b_router, fill=NEG_BIG)
    h2, u3, route, counts = _cross(h1, kv, norm_cross_g.reshape(1, d), w_cross_q.astype(BF16),
                                   w_cross_o.astype(BF16), norm_ffn_g.reshape(1, d), w_r, b_r, tm, s)

    idx = route[:, LANE_IDX:LANE_IDX + TOP_K].astype(jnp.int32)
    pos = route[:, LANE_POS:LANE_POS + TOP_K].astype(jnp.int32)
    cnt = counts[0, :N_EXPERTS].astype(jnp.int32)
    nblk = (cnt + EXPERT_ROWS - 1) // EXPERT_ROWS
    bend = jnp.cumsum(nblk)
    bstart = bend - nblk
    n_used = bend[-1:]
    dest = (bstart[idx] * EXPERT_ROWS + pos).reshape(n * TOP_K)
    nb = n * TOP_K // EXPERT_ROWS + N_EXPERTS
    blk = jnp.minimum(jnp.arange(nb, dtype=jnp.int32), n_used[0] - 1)
    blk_e = jnp.minimum(jnp.sum(blk[:, None] >= bend[None, :], axis=1), N_EXPERTS - 1).astype(jnp.int32)
    tail = n_used[0] + jnp.arange(N_EXPERTS, dtype=jnp.int32)
    zero_blk = jnp.concatenate([jnp.where(nblk > 0, bend - 1, -1),
                                jnp.where(tail < nb, tail, -1)]).astype(jnp.int32)

    td = min(ROW_MOVE_TOKENS, n)
    x_rows = _dispatch(zero_blk, dest, u3, nb * EXPERT_ROWS, td)
    y_rows = _experts(blk_e, n_used.astype(jnp.int32), x_rows, w_expert_in,
                      b_expert_in.reshape(N_EXPERTS, 1, -1), w_expert_out,
                      b_expert_out.reshape(N_EXPERTS, 1, -1))
    out = _combine(dest, y_rows, route, h2, final_g.reshape(1, d), td)
    return out.reshape(b, s, d)


def kernel(x, mem, norm_mix_g, w_in, conv_w, conv_b, dt_bias, a_log, d_skip, ssd_norm_g, w_ssd_branch,
           b_forget, w_fox_branch, b_gates, w_out, norm_cross_g, norm_mem_g, w_cross_q, w_cross_kv,
           w_cross_o, norm_ffn_g, w_router, b_router, w_expert_in, b_expert_in, w_expert_out,
           b_expert_out, norm_final_g):
    assert x.shape[2] == FOX_HEADS * FOX_HEAD_DIM == SSD_HEADS * SSD_HEAD_DIM
    assert norm_mix_g.shape[0] == 1, "single-layer problem"
    l = 0
    return _layer(x, mem, norm_mix_g[l], w_in[l], conv_w[l], conv_b[l], dt_bias[l], a_log[l], d_skip[l],
                  ssd_norm_g[l], w_ssd_branch[l], b_forget[l], w_fox_branch[l], b_gates[l], w_out[l],
                  norm_cross_g[l], norm_mem_g[l], w_cross_q[l], w_cross_kv[l], w_cross_o[l],
                  norm_ffn_g[l], w_router[l], b_router[l], w_expert_in[l], b_expert_in[l],
                  w_expert_out[l], b_expert_out[l], norm_final_g)
```

```python
import functools

import jax
import jax.numpy as jnp
from jax import lax
from jax.experimental import pallas as pl
from jax.experimental.pallas import tpu as pltpu

F32 = jnp.float32
BF16 = jnp.bfloat16

NORM_EPS = 1e-6
SSD_HEAD_DIM = 64
SSD_HEADS = 16
SSD_GROUPS = 4
SSD_STATE = 128
SSD_CONV = 4
SSD_CHUNK = 128
FOX_HEADS = 8
FOX_HEAD_DIM = 128
X_HEADS = 4
N_EXPERTS = 32
TOP_K = 4
SWIGLU_ALPHA = 1.702
SWIGLU_LIMIT = 7.0

LANES = 128
CONV_HALO = 8
ROW_SUB = 8
EXPERT_ROWS = 512
SSD_STEP_ROWS = 512
ROW_MOVE_TOKENS = 1024
NEG_BIG = -1e30
LOG2E = 1.4426950408889634
FOX_TQ, FOX_TK = 1024, 512
VMEM_LIMIT = 56 * 1024 * 1024
EXPERT_VMEM_LIMIT = 60 * 1024 * 1024

COL_XBC, COL_Z, COL_Q, COL_K, COL_V, COL_GATE, MAIN_COLS = 0, 2048, 3072, 4096, 5120, 6144, 8192
LANE_DT, LANE_F = 0, 16
LANE_IDX, LANE_POS, LANE_GATE = 0, 4, 8


def _sigmoid(x):
    return 1.0 / (1.0 + jnp.exp(-x))


def _rms(x, g):
    ms = jnp.mean(x * x, axis=-1, keepdims=True)
    return x * lax.rsqrt(ms + NORM_EPS) * g


def _dot(a, b):
    return jnp.dot(a, b, preferred_element_type=F32)


def _dot_nt(a, b):
    return lax.dot_general(a, b, (((1,), (1,)), ((), ())), preferred_element_type=F32)


def _split_dot(v, m_bf16, terms):
    out = None
    for _ in range(terms):
        hi = v.astype(BF16)
        part = _dot(hi, m_bf16)
        out = part if out is None else out + part
        v = v - hi.astype(F32)
    return out


def _store_rows(ref, x):
    n = x.shape[0]
    for sb in range(ROW_SUB):
        ref[pl.ds(sb, n, stride=ROW_SUB), :] = x[:, sb * LANES:(sb + 1) * LANES]


def _load_rows(ref, n, lead=()):
    return jnp.concatenate(
        [ref[lead + (pl.ds(sb, n, stride=ROW_SUB), slice(None))] for sb in range(ROW_SUB)], axis=1)


def _params(sem, vmem=VMEM_LIMIT):
    return pltpu.CompilerParams(dimension_semantics=sem, vmem_limit_bytes=vmem)


def _inproj_kernel(x_ref, g_ref, wm_ref, ws_ref, main_ref, small_ref, u_ref):
    @pl.when(pl.program_id(1) == 0)
    def _():
        u = _rms(x_ref[...], g_ref[...]).astype(BF16)
        u_ref[...] = u
        small_ref[...] = _dot(u, ws_ref[...])

    main_ref[...] = _dot(u_ref[...], wm_ref[...]).astype(BF16)


def _inproj(x2, g, w_main, w_small, tm, tn):
    n, d = x2.shape
    return pl.pallas_call(
        _inproj_kernel,
        grid=(n // tm, MAIN_COLS // tn),
        in_specs=[
            pl.BlockSpec((tm, d), lambda i, j: (i, 0)),
            pl.BlockSpec((1, d), lambda i, j: (0, 0)),
            pl.BlockSpec((d, tn), lambda i, j: (0, j)),
            pl.BlockSpec((d, LANES), lambda i, j: (0, 0)),
        ],
        out_specs=[
            pl.BlockSpec((tm, tn), lambda i, j: (i, j)),
            pl.BlockSpec((tm, LANES), lambda i, j: (i, 0)),
        ],
        out_shape=[jax.ShapeDtypeStruct((n, MAIN_COLS), BF16), jax.ShapeDtypeStruct((n, LANES), F32)],
        scratch_shapes=[pltpu.VMEM((tm, d), BF16)],
        compiler_params=_params(("arbitrary", "arbitrary")),
        name="inproj",
    )(x2, g, w_main, w_small)


def _ssd_kernel(xbc_ref, z_ref, sm_ref, cw_ref, cb_ref, bias_ref, alog_ref, dsk_ref, ng_ref, e_ref,
                y_ref, ct_ref, state, buf, ccarry):
    L = SSD_CHUNK

    @pl.when(pl.program_id(1) == 0)
    def _():
        state[...] = jnp.zeros_like(state)
        buf[0:CONV_HALO, :] = jnp.zeros((CONV_HALO, buf.shape[1]), F32)
        ccarry[...] = jnp.zeros_like(ccarry)

    for sub in range(xbc_ref.shape[1] // L):
        rows = slice(sub * L, (sub + 1) * L)
        y, ct = _ssd_chunk(xbc_ref[0, rows, :], z_ref[0, rows, :], sm_ref[0, rows, :], cw_ref, cb_ref, bias_ref,
                           alog_ref, dsk_ref, ng_ref, e_ref, state, buf, ccarry)
        y_ref[0, rows, :] = y
        ct_ref[0, :, rows] = ct


def _ssd_chunk(xbc_t, z_t, sm_t, cw_ref, cb_ref, bias_ref, alog_ref, dsk_ref, ng_ref, e_ref, state, buf, ccarry):
    L = SSD_CHUNK
    inner = SSD_HEADS * SSD_HEAD_DIM
    gw = inner // SSD_GROUPS
    hpg = SSD_HEADS // SSD_GROUPS

    xbc = xbc_t.astype(F32)
    buf[CONV_HALO:CONV_HALO + L, :] = xbc
    acc = cb_ref[...] + cw_ref[SSD_CONV - 1:SSD_CONV, :] * xbc
    for j in range(SSD_CONV - 1):
        acc = acc + cw_ref[j:j + 1, :] * buf[pl.ds(CONV_HALO - (SSD_CONV - 1) + j, L), :]
    buf[0:CONV_HALO, :] = xbc[L - CONV_HALO:L, :]
    xc = acc * _sigmoid(acc)
    xs = xc[:, :inner]
    bm = xc[:, inner:inner + SSD_GROUPS * SSD_STATE]
    cm = xc[:, inner + SSD_GROUPS * SSD_STATE:]

    sm = sm_t + bias_ref[...]
    t = jnp.log1p(jnp.exp(-jnp.abs(sm)))
    dt = jnp.maximum(sm, 0.0) + t
    logf = jnp.minimum(sm, 0.0) - t
    lane = lax.broadcasted_iota(jnp.int32, (L, LANES), 1)
    is_dt = lane < LANE_F
    a = -jnp.exp(alog_ref[...]) * dt
    comb = jnp.where(is_dt, a, jnp.where(lane < LANE_F + FOX_HEADS, logf, 0.0))
    row = lax.broadcasted_iota(jnp.int32, (L, L), 0)
    col = lax.broadcasted_iota(jnp.int32, (L, L), 1)
    causal = row >= col
    tril = jnp.where(causal, 1.0, 0.0).astype(BF16)
    cum = _split_dot_left(tril, comb)
    cglob = cum + ccarry[...]
    ccarry[...] = cglob[L - 1:L, :]
    ct_out = (cglob * LOG2E).T

    acum = jnp.where(is_dt, cum, 0.0)
    act = acum.T
    alast = acum[L - 1:L, :]
    e_mat = e_ref[...]
    dt_e = _split_dot(jnp.where(is_dt, dt, 0.0), e_mat, 2)
    ds_e = _split_dot(jnp.where(is_dt, jnp.exp(alast - acum), 0.0), e_mat, 2)
    ea_e = _split_dot(jnp.where(is_dt, jnp.exp(acum), 0.0), e_mat, 2)
    cd_e = _split_dot(jnp.broadcast_to(jnp.where(is_dt[0:1], jnp.exp(alast), 0.0), (8, LANES)), e_mat, 2)[0:1]

    xdt = xs * dt_e
    xdt_b = xdt.astype(BF16)
    xds_b = (xdt * ds_e).astype(BF16)
    glane = lax.broadcasted_iota(jnp.int32, (L, gw), 1)
    ys = []
    for g in range(SSD_GROUPS):
        gs = slice(g * gw, (g + 1) * gw)
        bg = bm[:, g * SSD_STATE:(g + 1) * SSD_STATE]
        cg = cm[:, g * SSD_STATE:(g + 1) * SSD_STATE].astype(BF16)
        cb = _dot_nt(cg, bg.astype(BF16))
        xg = xdt_b[:, gs]
        ydiag = jnp.zeros((L, gw), F32)
        for e in range(hpg):
            h = g * hpg + e
            diff = acum[:, h:h + 1] - act[h:h + 1, :]
            dm = jnp.exp(jnp.where(causal, diff, -jnp.inf))
            yfull = _dot((cb * dm).astype(BF16), xg)
            ydiag = jnp.where((glane >= e * SSD_HEAD_DIM) & (glane < (e + 1) * SSD_HEAD_DIM), yfull, ydiag)
        stg = state[:, gs]
        yoff = _dot(cg, stg.astype(BF16)) * ea_e[:, gs]
        state[:, gs] = stg * cd_e[:, gs] + _dot(bg.T.astype(BF16), xds_b[:, gs])
        ys.append(ydiag + yoff)
    y = jnp.concatenate(ys, axis=1) + dsk_ref[...] * xs

    z = z_t.astype(F32)
    gated = y * (z * _sigmoid(z))
    outs = []
    for g in range(SSD_GROUPS):
        seg = gated[:, g * gw:(g + 1) * gw]
        ms = jnp.mean(seg * seg, axis=-1, keepdims=True)
        outs.append(seg * lax.rsqrt(ms + NORM_EPS))
    return (jnp.concatenate(outs, axis=1) * ng_ref[...]).astype(BF16), ct_out


def _split_dot_left(m_bf16, v):
    out = None
    for _ in range(3):
        hi = v.astype(BF16)
        part = _dot(m_bf16, hi)
        out = part if out is None else out + part
        v = v - hi.astype(F32)
    return out


def _ssd(main3, small3, conv_w, conv_b, bias_row, alog_row, dskip_row, normg_row, e_mat):
    b, s, _ = main3.shape
    L = SSD_CHUNK
    inner = SSD_HEADS * SSD_HEAD_DIM
    cc = conv_w.shape[1]
    const = lambda shape: pl.BlockSpec(shape, lambda i, c: (0,) * len(shape))
    r = min(SSD_STEP_ROWS, s)
    return pl.pallas_call(
        _ssd_kernel,
        grid=(b, s // r),
        in_specs=[
            pl.BlockSpec((1, r, cc), lambda i, c: (i, c, COL_XBC // cc)),
            pl.BlockSpec((1, r, inner), lambda i, c: (i, c, COL_Z // inner)),
            pl.BlockSpec((1, r, LANES), lambda i, c: (i, c, 0)),
            const((SSD_CONV, cc)), const((1, cc)), const((1, LANES)), const((1, LANES)),
            const((1, inner)), const((1, inner)), const((LANES, inner)),
        ],
        out_specs=[
            pl.BlockSpec((1, r, inner), lambda i, c: (i, c, 0)),
            pl.BlockSpec((1, LANES, r), lambda i, c: (i, 0, c)),
        ],
        out_shape=[
            jax.ShapeDtypeStruct((b, s, inner), BF16),
            jax.ShapeDtypeStruct((b, LANES, s), F32),
        ],
        scratch_shapes=[
            pltpu.VMEM((SSD_STATE, inner), F32),
            pltpu.VMEM((L + CONV_HALO, cc), F32),
            pltpu.VMEM((1, LANES), F32),
        ],
        compiler_params=_params(("arbitrary", "arbitrary")),
        name="ssd",
    )(main3, main3, small3, conv_w, conv_b, bias_row, alog_row, dskip_row, normg_row, e_mat)


def _fox_kernel(q_ref, k_ref, v_ref, ck_ref, o_ref, m_sc, l_sc, acc_sc, *, tq, tk):
    h = pl.program_id(1)
    i = pl.program_id(2)
    m_sc[...] = jnp.full(m_sc.shape, NEG_BIG, F32)
    l_sc[...] = jnp.zeros_like(l_sc)
    acc_sc[...] = jnp.zeros_like(acc_sc)
    reps = tk // LANES

    def step(j, masked, r0):
        rows = slice(r0, tq)
        ks = pl.multiple_of(j * tk, tk)
        k = k_ref[0, pl.ds(ks, tk), :]
        v = v_ref[0, pl.ds(ks, tk), :]
        s = _dot_nt(q_ref[0, rows, :], k) - ck_ref[0, pl.ds(h, 1), pl.ds(ks, tk)]
        if masked:
            row = i * tq + r0 + lax.broadcasted_iota(jnp.int32, (tq - r0, tk), 0)
            col = j * tk + lax.broadcasted_iota(jnp.int32, (tq - r0, tk), 1)
            s = jnp.where(row >= col, s, -jnp.inf)
        m_prev = m_sc[rows, :]
        m_new = jnp.maximum(m_prev, jnp.max(s, axis=-1, keepdims=True))
        alpha = jnp.exp2(m_prev - m_new)
        p = jnp.exp2(s - jnp.concatenate([m_new] * reps, axis=1))
        l_sc[rows, :] = alpha * l_sc[rows, :] + jnp.sum(p, axis=-1, keepdims=True)
        acc_sc[rows, :] = alpha * acc_sc[rows, :] + _dot(p.astype(BF16), v)
        m_sc[rows, :] = m_new

    def body(j, carry):
        step(j, False, 0)
        return carry

    if tk >= tq:
        n_full, n_diag = (i * tq) // tk, 1
    else:
        n_full, n_diag = i * (tq // tk), tq // tk
    lax.fori_loop(0, n_full, body, 0)
    for jd in range(n_diag):
        step(n_full + jd, True, jd * tk if tk < tq else 0)
    o_ref[0] = (acc_sc[...] / l_sc[...]).astype(BF16)


def _fox(main3, ct3, tq, tk):
    b, s, _ = main3.shape
    dh = FOX_HEAD_DIM
    assert dh == LANES and (tk % tq == 0 or tq % tk == 0)
    return pl.pallas_call(
        functools.partial(_fox_kernel, tq=tq, tk=tk),
        grid=(b, FOX_HEADS, s // tq),
        in_specs=[
            pl.BlockSpec((1, tq, dh), lambda bi, h, i: (bi, i, COL_Q // dh + h)),
            pl.BlockSpec((1, s, dh), lambda bi, h, i: (bi, 0, COL_K // dh + h)),
            pl.BlockSpec((1, s, dh), lambda bi, h, i: (bi, 0, COL_V // dh + h)),
            pl.BlockSpec((1, FOX_HEADS, s), lambda bi, h, i: (bi, LANE_F // FOX_HEADS, 0)),
        ],
        out_specs=pl.BlockSpec((1, tq, dh), lambda bi, h, i: (bi, i, h)),
        out_shape=jax.ShapeDtypeStruct((b, s, FOX_HEADS * dh), BF16),
        scratch_shapes=[pltpu.VMEM((tq, LANES), F32), pltpu.VMEM((tq, LANES), F32), pltpu.VMEM((tq, dh), F32)],
        compiler_params=_params(("arbitrary", "arbitrary", "arbitrary")),
        name="fox",
    )(main3, main3, main3, ct3)


def _memkv_kernel(mem_ref, g_ref, w_ref, kv_ref):
    kv_ref[0] = _dot(_rms(mem_ref[0], g_ref[...]).astype(BF16), w_ref[...]).astype(BF16)


def _memkv(mem, g, w_kv):
    b, m, d = mem.shape
    return pl.pallas_call(
        _memkv_kernel,
        grid=(b,),
        in_specs=[
            pl.BlockSpec((1, m, d), lambda i: (i, 0, 0)),
            pl.BlockSpec((1, d), lambda i: (0, 0)),
            pl.BlockSpec((d, 2 * d), lambda i: (0, 0)),
        ],
        out_specs=pl.BlockSpec((1, m, 2 * d), lambda i: (i, 0, 0)),
        out_shape=jax.ShapeDtypeStruct((b, m, 2 * d), BF16),
        compiler_params=_params(("arbitrary",)),
        name="memkv",
    )(mem, g, w_kv)


def _cross_kernel(ys_ref, yf_ref, gt_ref, bg_ref, x_ref, ws_ref, wf_ref, wm_ref,
                  kv_ref, gc_ref, wq_ref, wo_ref, gf_ref, wr_ref, br_ref,
                  h2_ref, u3_ref, route_ref, cnt_ref, base):
    tm, d = x_ref.shape
    dh = d // X_HEADS

    @pl.when(pl.program_id(0) == 0)
    def _():
        base[...] = jnp.zeros_like(base)

    g = _sigmoid(gt_ref[...].astype(F32) + bg_ref[...])
    merged = g[:, :d] * _dot(ys_ref[...], ws_ref[...]) + g[:, d:] * _dot(yf_ref[...], wf_ref[...])
    h1 = x_ref[...] + _dot(merged.astype(BF16), wm_ref[...])

    q = _dot(_rms(h1, gc_ref[...]).astype(BF16), wq_ref[...]).astype(BF16)
    outs = []
    for hd in range(X_HEADS):
        kh = kv_ref[0, :, hd * dh:(hd + 1) * dh]
        vh = kv_ref[0, :, d + hd * dh:d + (hd + 1) * dh]
        s = _dot_nt(q[:, hd * dh:(hd + 1) * dh], kh) * (dh ** -0.5)
        p = jnp.exp(s - jnp.max(s, axis=-1, keepdims=True))
        outs.append(_dot(p.astype(BF16), vh) / jnp.sum(p, axis=-1, keepdims=True))
    h2 = h1 + _dot(jnp.concatenate(outs, axis=1).astype(BF16), wo_ref[...])
    h2_ref[...] = h2
    u3 = _rms(h2, gf_ref[...])
    _store_rows(u3_ref, u3)

    u_hi = u3.astype(BF16)
    u_lo = (u3 - u_hi.astype(F32)).astype(BF16)
    w_r = wr_ref[...]
    w_hi = w_r.astype(BF16)
    w_lo = (w_r - w_hi.astype(F32)).astype(BF16)
    logits = _dot(u_hi, w_hi) + _dot(u_hi, w_lo) + _dot(u_lo, w_hi) + br_ref[...]
    lane = lax.broadcasted_iota(jnp.int32, (tm, LANES), 1)
    lane_f = lane.astype(F32)
    vals, idxs = [], []
    work = logits
    for _ in range(TOP_K):
        m = jnp.max(work, axis=-1, keepdims=True)
        ix = jnp.min(jnp.where(work == m, lane_f, float(LANES)), axis=-1, keepdims=True)
        vals.append(m)
        idxs.append(ix)
        work = jnp.where(lane_f == ix, -jnp.inf, work)
    exps = [jnp.exp(v - vals[0]) for v in vals]
    den = exps[0] + exps[1] + exps[2] + exps[3]

    onehot = jnp.zeros((tm, LANES), F32)
    for ix in idxs:
        onehot = onehot + jnp.where(lane_f == ix, 1.0, 0.0)
    row = lax.broadcasted_iota(jnp.int32, (tm, tm), 0)
    col = lax.broadcasted_iota(jnp.int32, (tm, tm), 1)
    strict = jnp.where(row > col, 1.0, 0.0).astype(BF16)
    rank = _dot(strict, onehot.astype(BF16)) + base[...]
    route = jnp.zeros((tm, LANES), F32)
    for r in range(TOP_K):
        pos = jnp.sum(jnp.where(lane_f == idxs[r], rank, 0.0), axis=-1, keepdims=True)
        route = jnp.where(lane == LANE_IDX + r, idxs[r], route)
        route = jnp.where(lane == LANE_POS + r, pos, route)
        route = jnp.where(lane == LANE_GATE + r, exps[r] / den, route)
    route_ref[...] = route
    new_base = base[...] + jnp.sum(onehot, axis=0, keepdims=True)
    base[...] = new_base
    cnt_ref[...] = new_base


def _cross(y_ssd, y_fox, main2, b_gates, x2, w_s, w_f, w_m, kv, g_cross, w_q, w_o, g_ffn, w_r, b_r, tm, seq):
    n, d = x2.shape
    m = kv.shape[1]
    per_batch = seq // tm
    const = lambda shape: pl.BlockSpec(shape, lambda i: (0,) * len(shape))
    return pl.pallas_call(
        _cross_kernel,
        grid=(n // tm,),
        in_specs=[
            pl.BlockSpec((tm, d), lambda i: (i, 0)),
            pl.BlockSpec((tm, d), lambda i: (i, 0)),
            pl.BlockSpec((tm, 2 * d), lambda i: (i, COL_GATE // (2 * d))),
            const((1, 2 * d)),
            pl.BlockSpec((tm, d), lambda i: (i, 0)),
            const((d, d)), const((d, d)), const((d, d)),
            pl.BlockSpec((1, m, 2 * d), lambda i: (i // per_batch, 0, 0)),
            const((1, d)), const((d, d)), const((d, d)), const((1, d)),
            const((d, LANES)), const((1, LANES)),
        ],
        out_specs=[
            pl.BlockSpec((tm, d), lambda i: (i, 0)),
            pl.BlockSpec((tm * ROW_SUB, LANES), lambda i: (i, 0)),
            pl.BlockSpec((tm, LANES), lambda i: (i, 0)),
            pl.BlockSpec((1, LANES), lambda i: (0, 0)),
        ],
        out_shape=[
            jax.ShapeDtypeStruct((n, d), F32),
            jax.ShapeDtypeStruct((n * ROW_SUB, LANES), F32),
            jax.ShapeDtypeStruct((n, LANES), F32),
            jax.ShapeDtypeStruct((1, LANES), F32),
        ],
        scratch_shapes=[pltpu.VMEM((1, LANES), F32)],
        compiler_params=_params(("arbitrary",)),
        name="cross_router",
    )(y_ssd, y_fox, main2, b_gates, x2, w_s, w_f, w_m, kv, g_cross, w_q, w_o, g_ffn, w_r, b_r)


def _dispatch_kernel(zb_ref, dest_ref, u_ref, xr_ref, zbuf, sem, zsem, *, td):
    i = pl.program_id(0)

    @pl.when(i == 0)
    def _():
        zbuf[...] = jnp.zeros_like(zbuf)

        def zero_copy(e):
            blk = EXPERT_ROWS * ROW_SUB
            return pltpu.make_async_copy(zbuf, xr_ref.at[pl.ds(pl.multiple_of(zb_ref[e] * blk, blk), blk)], zsem)

        for e in range(2 * N_EXPERTS):
            @pl.when(zb_ref[e] >= 0)
            def _():
                zero_copy(e).start()
        for e in range(2 * N_EXPERTS):
            @pl.when(zb_ref[e] >= 0)
            def _():
                zero_copy(e).wait()

    def issue(t, carry):
        for k in range(TOP_K):
            pltpu.make_async_copy(_row_tile(u_ref, t), _row_tile(xr_ref, dest_ref[t * TOP_K + k]),
                                  sem).start(priority=k % 2)
        return carry

    lax.fori_loop(0, td, issue, 0, unroll=4)
    for k in range(TOP_K):
        pltpu.make_async_copy(u_ref, xr_ref.at[pl.ds(0, td * ROW_SUB)], sem).wait()


def _row_tile(ref, r, lead=()):
    return ref.at[lead + (pl.ds(pl.multiple_of(r * ROW_SUB, ROW_SUB), ROW_SUB),)]


def _dispatch(zero_blk, dest_flat, u3, n_rows, td):
    n = u3.shape[0] // ROW_SUB
    return pl.pallas_call(
        functools.partial(_dispatch_kernel, td=td),
        grid_spec=pltpu.PrefetchScalarGridSpec(
            num_scalar_prefetch=1,
            grid=(n // td,),
            in_specs=[
                pl.BlockSpec((td * TOP_K,), lambda i, zb: (i,), memory_space=pltpu.SMEM),
                pl.BlockSpec((td * ROW_SUB, LANES), lambda i, zb: (i, 0)),
            ],
            out_specs=pl.BlockSpec(memory_space=pl.ANY),
            scratch_shapes=[
                pltpu.VMEM((EXPERT_ROWS * ROW_SUB, LANES), F32),
                pltpu.SemaphoreType.DMA(()),
                pltpu.SemaphoreType.DMA(()),
            ],
        ),
        out_shape=jax.ShapeDtypeStruct((n_rows * ROW_SUB, LANES), F32),
        compiler_params=_params(("arbitrary",)),
        name="dispatch",
    )(zero_blk, dest_flat, u3)


def _expert_kernel(be_ref, nu_ref, x_ref, w1_ref, b1_ref, w2_ref, b2_ref, y_ref, w1b, w2b):
    f = w2_ref.shape[1]
    i = pl.program_id(0)
    used = i < nu_ref[0]

    @pl.when(jnp.logical_and(used, jnp.logical_or(i == 0, be_ref[i] != be_ref[jnp.maximum(i - 1, 0)])))
    def _():
        w1b[...] = w1_ref[0].astype(BF16)
        w2b[...] = w2_ref[0].astype(BF16)

    @pl.when(used)
    def _():
        hgl = _dot(_load_rows(x_ref, EXPERT_ROWS).astype(BF16), w1b[...]) + b1_ref[0]
        glu = jnp.minimum(hgl[:, :f], SWIGLU_LIMIT)
        lin = jnp.clip(hgl[:, f:], -SWIGLU_LIMIT, SWIGLU_LIMIT)
        act = glu * _sigmoid(SWIGLU_ALPHA * glu) * (lin + 1.0)
        _store_rows(y_ref, _dot(act.astype(BF16), w2b[...]) + b2_ref[0])

    @pl.when(jnp.logical_not(used))
    def _():
        y_ref[...] = jnp.zeros_like(y_ref)


def _experts(blk_e, n_used, x_rows, w1, b1, w2, b2):
    f, d = w2.shape[1], w2.shape[2]
    blk = EXPERT_ROWS * ROW_SUB
    nb = x_rows.shape[0] // blk
    row_blk = lambda i, be, nu: (jnp.minimum(i, nu[0] - 1), 0)
    return pl.pallas_call(
        _expert_kernel,
        grid_spec=pltpu.PrefetchScalarGridSpec(
            num_scalar_prefetch=2,
            grid=(nb,),
            in_specs=[
                pl.BlockSpec((blk, LANES), row_blk),
                pl.BlockSpec((1, d, 2 * f), lambda i, be, nu: (be[i], 0, 0)),
                pl.BlockSpec((1, 1, 2 * f), lambda i, be, nu: (be[i], 0, 0)),
                pl.BlockSpec((1, f, d), lambda i, be, nu: (be[i], 0, 0)),
                pl.BlockSpec((1, 1, d), lambda i, be, nu: (be[i], 0, 0)),
            ],
            out_specs=pl.BlockSpec((blk, LANES), lambda i, be, nu: (i, 0)),
            scratch_shapes=[pltpu.VMEM((d, 2 * f), BF16), pltpu.VMEM((f, d), BF16)],
        ),
        out_shape=jax.ShapeDtypeStruct(x_rows.shape, F32),
        compiler_params=_params(("arbitrary",), EXPERT_VMEM_LIMIT),
        name="experts",
    )(blk_e, n_used, x_rows, w1, b1, w2, b2)


def _combine_kernel(dest_ref, y_ref, route_ref, h_ref, g_ref, o_ref, ybuf, sem, *, tc):
    def issue(t, carry):
        for k in range(TOP_K):
            pltpu.make_async_copy(_row_tile(y_ref, dest_ref[t * TOP_K + k]), _row_tile(ybuf, t, (k,)),
                                  sem).start(priority=k % 2)
        return carry

    lax.fori_loop(0, tc, issue, 0, unroll=4)
    for k in range(TOP_K):
        pltpu.make_async_copy(y_ref.at[pl.ds(0, tc * ROW_SUB)], ybuf.at[k], sem).wait()

    route = route_ref[...]
    lane = lax.broadcasted_iota(jnp.int32, route.shape, 1)
    acc = h_ref[...]
    for k in range(TOP_K):
        gate = jnp.sum(jnp.where(lane == LANE_GATE + k, route, 0.0), axis=-1, keepdims=True)
        acc = acc + gate * _load_rows(ybuf, tc, (k,))
    o_ref[...] = _rms(acc, g_ref[...])


def _combine(dest_flat, y_rows, route, h2, g_final, tc):
    n, d = h2.shape
    return pl.pallas_call(
        functools.partial(_combine_kernel, tc=tc),
        grid=(n // tc,),
        in_specs=[
            pl.BlockSpec((tc * TOP_K,), lambda i: (i,), memory_space=pltpu.SMEM),
            pl.BlockSpec(memory_space=pl.ANY),
            pl.BlockSpec((tc, LANES), lambda i: (i, 0)),
            pl.BlockSpec((tc, d), lambda i: (i, 0)),
            pl.BlockSpec((1, d), lambda i: (0, 0)),
        ],
        out_specs=pl.BlockSpec((tc, d), lambda i: (i, 0)),
        out_shape=jax.ShapeDtypeStruct((n, d), F32),
        scratch_shapes=[pltpu.VMEM((TOP_K, tc * ROW_SUB, LANES), F32), pltpu.SemaphoreType.DMA(())],
        compiler_params=_params(("arbitrary",)),
        name="combine",
    )(dest_flat, y_rows, route, h2, g_final)


def _pad_lanes(v, offset=0, fill=0.0):
    out = jnp.full((1, LANES), fill, F32)
    return lax.dynamic_update_slice(out, v.reshape(1, -1).astype(F32), (0, offset))


def _layer(h, mem, norm_mix_g, w_in, conv_w, conv_b, dt_bias, a_log, d_skip, ssd_norm_g,
           w_ssd_branch, b_forget, w_fox_branch, b_gates, w_out, norm_cross_g, norm_mem_g,
           w_cross_q, w_cross_kv, w_cross_o, norm_ffn_g, w_router, b_router,
           w_expert_in, b_expert_in, w_expert_out, b_expert_out, final_g):
    b, s, d = h.shape
    n = b * s
    inner = SSD_HEADS * SSD_HEAD_DIM
    cc = inner + 2 * SSD_GROUPS * SSD_STATE
    tm = min(512, s)

    cuts = [inner, cc, SSD_HEADS, d, d, d, FOX_HEADS, 2 * d]
    offs = [0]
    for c in cuts:
        offs.append(offs[-1] + c)
    wz, wxbc, wdt, wq, wk, wv, wf, wg = [w_in[:, offs[i]:offs[i + 1]] for i in range(8)]
    w_main = jnp.concatenate([wxbc, wz, wq * (LOG2E * FOX_HEAD_DIM ** -0.5), wk, wv, wg], axis=1).astype(BF16)
    w_small = jnp.concatenate(
        [wdt, wf, jnp.zeros((d, LANES - SSD_HEADS - FOX_HEADS), F32)], axis=1).astype(BF16)

    x2 = h.reshape(n, d)
    main, small = _inproj(x2, norm_mix_g.reshape(1, d), w_main, w_small, min(1024, n), 2048)
    main3 = main.reshape(b, s, MAIN_COLS)

    bias_row = _pad_lanes(jnp.concatenate([dt_bias, b_forget]))
    alog_row = _pad_lanes(a_log)
    dskip_row = jnp.repeat(d_skip, SSD_HEAD_DIM).reshape(1, inner)
    e_mat = (jnp.arange(LANES)[:, None] == (jnp.arange(inner)[None, :] // SSD_HEAD_DIM)).astype(BF16)
    y_ssd, ct3 = _ssd(main3, small.reshape(b, s, LANES), conv_w, conv_b.reshape(1, cc), bias_row,
                      alog_row, dskip_row, ssd_norm_g.reshape(1, inner), e_mat)
    y_fox = _fox(main3, ct3, min(FOX_TQ, s), min(FOX_TK, s))

    kv = _memkv(mem, norm_mem_g.reshape(1, d), w_cross_kv.astype(BF16))
    w_r = jnp.concatenate([w_router, jnp.zeros((d, LANES - N_EXPERTS), F32)], axis=1)
    b_r = _pad_lanes(b_router, fill=NEG_BIG)
    h2, u3, route, counts = _cross(
        y_ssd.reshape(n, inner), y_fox.reshape(n, d), main, b_gates.reshape(1, 2 * d), x2,
        w_ssd_branch.astype(BF16), w_fox_branch.astype(BF16), w_out.astype(BF16),
        kv, norm_cross_g.reshape(1, d), w_cross_q.astype(BF16), w_cross_o.astype(BF16),
        norm_ffn_g.reshape(1, d), w_r, b_r, tm, s)

    idx = route[:, LANE_IDX:LANE_IDX + TOP_K].astype(jnp.int32)
    pos = route[:, LANE_POS:LANE_POS + TOP_K].astype(jnp.int32)
    cnt = counts[0, :N_EXPERTS].astype(jnp.int32)
    nblk = (cnt + EXPERT_ROWS - 1) // EXPERT_ROWS
    bend = jnp.cumsum(nblk)
    bstart = bend - nblk
    n_used = bend[-1:]
    dest = (bstart[idx] * EXPERT_ROWS + pos).reshape(n * TOP_K)
    nb = n * TOP_K // EXPERT_ROWS + N_EXPERTS
    blk = jnp.minimum(jnp.arange(nb, dtype=jnp.int32), n_used[0] - 1)
    blk_e = jnp.minimum(jnp.sum(blk[:, None] >= bend[None, :], axis=1), N_EXPERTS - 1).astype(jnp.int32)
    tail = n_used[0] + jnp.arange(N_EXPERTS, dtype=jnp.int32)
    zero_blk = jnp.concatenate([jnp.where(nblk > 0, bend - 1, -1),
                                jnp.where(tail < nb, tail, -1)]).astype(jnp.int32)

    td = min(ROW_MOVE_TOKENS, n)
    x_rows = _dispatch(zero_blk, dest, u3, nb * EXPERT_ROWS, td)
    y_rows = _experts(blk_e, n_used.astype(jnp.int32), x_rows, w_expert_in,
                      b_expert_in.reshape(N_EXPERTS, 1, -1), w_expert_out,
                      b_expert_out.reshape(N_EXPERTS, 1, -1))
    out = _combine(dest, y_rows, route, h2, final_g.reshape(1, d), td)
    return out.reshape(b, s, d)


def kernel(x, mem, norm_mix_g, w_in, conv_w, conv_b, dt_bias, a_log, d_skip, ssd_norm_g, w_ssd_branch,
           b_forget, w_fox_branch, b_gates, w_out, norm_cross_g, norm_mem_g, w_cross_q, w_cross_kv,
           w_cross_o, norm_ffn_g, w_router, b_router, w_expert_in, b_expert_in, w_expert_out,
           b_expert_out, norm_final_g):
    assert x.shape[2] == FOX_HEADS * FOX_HEAD_DIM == SSD_HEADS * SSD_HEAD_DIM
    assert norm_mix_g.shape[0] == 1, "single-layer problem"
    l = 0
    return _layer(x, mem, norm_mix_g[l], w_in[l], conv_w[l], conv_b[l], dt_bias[l], a_log[l], d_skip[l],
                  ssd_norm_g[l], w_ssd_branch[l], b_forget[l], w_fox_branch[l], b_gates[l], w_out[l],
                  norm_cross_g[l], norm_mem_g[l], w_cross_q[l], w_cross_kv[l], w_cross_o[l],
                  norm_ffn_g[l], w_router[l], b_router[l], w_expert_in[l], b_expert_in[l],
                  w_expert_out[l], b_expert_out[l], norm_final_g)
```

```python
import functools

import jax
import jax.numpy as jnp
from jax import lax
from jax.experimental import pallas as pl
from jax.experimental.pallas import tpu as pltpu

F32 = jnp.float32
BF16 = jnp.bfloat16

NORM_EPS = 1e-6
SSD_HEAD_DIM = 64
SSD_HEADS = 16
SSD_GROUPS = 4
SSD_STATE = 128
SSD_CONV = 4
SSD_CHUNK = 128
FOX_HEADS = 8
FOX_HEAD_DIM = 128
X_HEADS = 4
N_EXPERTS = 32
TOP_K = 4
SWIGLU_ALPHA = 1.702
SWIGLU_LIMIT = 7.0

LANES = 128
CONV_HALO = 8
ROW_SUB = 8
EXPERT_ROWS = 512
SSD_STEP_ROWS = 512
DISPATCH_TOKENS = 4096
COMBINE_TOKENS = 1024
NEG_BIG = -1e30
LOG2E = 1.4426950408889634
FOX_TQ, FOX_TK = 1024, 512
VMEM_LIMIT = 56 * 1024 * 1024
EXPERT_VMEM_LIMIT = 60 * 1024 * 1024

COL_XBC, COL_Z, COL_Q, COL_K, COL_V, COL_GATE, MAIN_COLS = 0, 2048, 3072, 4096, 5120, 6144, 8192
LANE_DT, LANE_F = 0, 16
LANE_IDX, LANE_POS, LANE_GATE = 0, 4, 8


def _sigmoid(x):
    return 1.0 / (1.0 + jnp.exp(-x))


def _rms(x, g):
    ms = jnp.mean(x * x, axis=-1, keepdims=True)
    return x * lax.rsqrt(ms + NORM_EPS) * g


def _dot(a, b):
    return jnp.dot(a, b, preferred_element_type=F32)


def _dot_nt(a, b):
    return lax.dot_general(a, b, (((1,), (1,)), ((), ())), preferred_element_type=F32)


def _split_dot(v, m_bf16, terms):
    out = None
    for _ in range(terms):
        hi = v.astype(BF16)
        part = _dot(hi, m_bf16)
        out = part if out is None else out + part
        v = v - hi.astype(F32)
    return out


def _store_rows(ref, x):
    n = x.shape[0]
    for sb in range(ROW_SUB):
        ref[pl.ds(sb, n, stride=ROW_SUB), :] = x[:, sb * LANES:(sb + 1) * LANES]


def _load_rows(ref, n, lead=()):
    return jnp.concatenate(
        [ref[lead + (pl.ds(sb, n, stride=ROW_SUB), slice(None))] for sb in range(ROW_SUB)], axis=1)


def _params(sem, vmem=VMEM_LIMIT):
    return pltpu.CompilerParams(dimension_semantics=sem, vmem_limit_bytes=vmem)


def _inproj_kernel(x_ref, g_ref, wm_ref, ws_ref, main_ref, small_ref, u_ref):
    @pl.when(pl.program_id(1) == 0)
    def _():
        u = _rms(x_ref[...], g_ref[...]).astype(BF16)
        u_ref[...] = u
        small_ref[...] = _dot(u, ws_ref[...])

    main_ref[...] = _dot(u_ref[...], wm_ref[...]).astype(BF16)


def _inproj(x2, g, w_main, w_small, tm, tn):
    n, d = x2.shape
    return pl.pallas_call(
        _inproj_kernel,
        grid=(n // tm, MAIN_COLS // tn),
        in_specs=[
            pl.BlockSpec((tm, d), lambda i, j: (i, 0)),
            pl.BlockSpec((1, d), lambda i, j: (0, 0)),
            pl.BlockSpec((d, tn), lambda i, j: (0, j)),
            pl.BlockSpec((d, LANES), lambda i, j: (0, 0)),
        ],
        out_specs=[
            pl.BlockSpec((tm, tn), lambda i, j: (i, j)),
            pl.BlockSpec((tm, LANES), lambda i, j: (i, 0)),
        ],
        out_shape=[jax.ShapeDtypeStruct((n, MAIN_COLS), BF16), jax.ShapeDtypeStruct((n, LANES), F32)],
        scratch_shapes=[pltpu.VMEM((tm, d), BF16)],
        compiler_params=_params(("arbitrary", "arbitrary")),
        name="inproj",
    )(x2, g, w_main, w_small)


def _ssd_kernel(xbc_ref, z_ref, sm_ref, cw_ref, cb_ref, bias_ref, alog_ref, dsk_ref, ng_ref, e_ref,
                y_ref, ct_ref, state, buf, ccarry):
    L = SSD_CHUNK

    @pl.when(pl.program_id(1) == 0)
    def _():
        state[...] = jnp.zeros_like(state)
        buf[0:CONV_HALO, :] = jnp.zeros((CONV_HALO, buf.shape[1]), F32)
        ccarry[...] = jnp.zeros_like(ccarry)

    for sub in range(xbc_ref.shape[1] // L):
        rows = slice(sub * L, (sub + 1) * L)
        y, ct = _ssd_chunk(xbc_ref[0, rows, :], z_ref[0, rows, :], sm_ref[0, rows, :], cw_ref, cb_ref, bias_ref,
                           alog_ref, dsk_ref, ng_ref, e_ref, state, buf, ccarry)
        y_ref[0, rows, :] = y
        ct_ref[0, :, rows] = ct


def _ssd_chunk(xbc_t, z_t, sm_t, cw_ref, cb_ref, bias_ref, alog_ref, dsk_ref, ng_ref, e_ref, state, buf, ccarry):
    L = SSD_CHUNK
    inner = SSD_HEADS * SSD_HEAD_DIM
    gw = inner // SSD_GROUPS
    hpg = SSD_HEADS // SSD_GROUPS

    xbc = xbc_t.astype(F32)
    buf[CONV_HALO:CONV_HALO + L, :] = xbc
    acc = cb_ref[...] + cw_ref[SSD_CONV - 1:SSD_CONV, :] * xbc
    for j in range(SSD_CONV - 1):
        acc = acc + cw_ref[j:j + 1, :] * buf[pl.ds(CONV_HALO - (SSD_CONV - 1) + j, L), :]
    buf[0:CONV_HALO, :] = xbc[L - CONV_HALO:L, :]
    xc = acc * _sigmoid(acc)
    xs = xc[:, :inner]
    bm = xc[:, inner:inner + SSD_GROUPS * SSD_STATE]
    cm = xc[:, inner + SSD_GROUPS * SSD_STATE:]

    sm = sm_t + bias_ref[...]
    t = jnp.log1p(jnp.exp(-jnp.abs(sm)))
    dt = jnp.maximum(sm, 0.0) + t
    logf = jnp.minimum(sm, 0.0) - t
    lane = lax.broadcasted_iota(jnp.int32, (L, LANES), 1)
    is_dt = lane < LANE_F
    a = -jnp.exp(alog_ref[...]) * dt
    comb = jnp.where(is_dt, a, jnp.where(lane < LANE_F + FOX_HEADS, logf, 0.0))
    row = lax.broadcasted_iota(jnp.int32, (L, L), 0)
    col = lax.broadcasted_iota(jnp.int32, (L, L), 1)
    causal = row >= col
    tril = jnp.where(causal, 1.0, 0.0).astype(BF16)
    cum = _split_dot_left(tril, comb)
    cglob = cum + ccarry[...]
    ccarry[...] = cglob[L - 1:L, :]
    ct_out = (cglob * LOG2E).T

    acum = jnp.where(is_dt, cum, 0.0)
    act = acum.T
    alast = acum[L - 1:L, :]
    e_mat = e_ref[...]
    dt_e = _split_dot(jnp.where(is_dt, dt, 0.0), e_mat, 2)
    ds_e = _split_dot(jnp.where(is_dt, jnp.exp(alast - acum), 0.0), e_mat, 2)
    ea_e = _split_dot(jnp.where(is_dt, jnp.exp(acum), 0.0), e_mat, 2)
    cd_e = _split_dot(jnp.broadcast_to(jnp.where(is_dt[0:1], jnp.exp(alast), 0.0), (8, LANES)), e_mat, 2)[0:1]

    xdt = xs * dt_e
    xdt_b = xdt.astype(BF16)
    xds_b = (xdt * ds_e).astype(BF16)
    glane = lax.broadcasted_iota(jnp.int32, (L, gw), 1)
    ys = []
    for g in range(SSD_GROUPS):
        gs = slice(g * gw, (g + 1) * gw)
        bg = bm[:, g * SSD_STATE:(g + 1) * SSD_STATE]
        cg = cm[:, g * SSD_STATE:(g + 1) * SSD_STATE].astype(BF16)
        cb = _dot_nt(cg, bg.astype(BF16))
        xg = xdt_b[:, gs]
        ydiag = jnp.zeros((L, gw), F32)
        for e in range(hpg):
            h = g * hpg + e
            diff = acum[:, h:h + 1] - act[h:h + 1, :]
            dm = jnp.exp(jnp.where(causal, diff, -jnp.inf))
            yfull = _dot((cb * dm).astype(BF16), xg)
            ydiag = jnp.where((glane >= e * SSD_HEAD_DIM) & (glane < (e + 1) * SSD_HEAD_DIM), yfull, ydiag)
        stg = state[:, gs]
        yoff = _dot(cg, stg.astype(BF16)) * ea_e[:, gs]
        state[:, gs] = stg * cd_e[:, gs] + _dot(bg.T.astype(BF16), xds_b[:, gs])
        ys.append(ydiag + yoff)
    y = jnp.concatenate(ys, axis=1) + dsk_ref[...] * xs

    z = z_t.astype(F32)
    gated = y * (z * _sigmoid(z))
    outs = []
    for g in range(SSD_GROUPS):
        seg = gated[:, g * gw:(g + 1) * gw]
        ms = jnp.mean(seg * seg, axis=-1, keepdims=True)
        outs.append(seg * lax.rsqrt(ms + NORM_EPS))
    return (jnp.concatenate(outs, axis=1) * ng_ref[...]).astype(BF16), ct_out


def _split_dot_left(m_bf16, v):
    out = None
    for _ in range(3):
        hi = v.astype(BF16)
        part = _dot(m_bf16, hi)
        out = part if out is None else out + part
        v = v - hi.astype(F32)
    return out


def _ssd(main3, small3, conv_w, conv_b, bias_row, alog_row, dskip_row, normg_row, e_mat):
    b, s, _ = main3.shape
    L = SSD_CHUNK
    inner = SSD_HEADS * SSD_HEAD_DIM
    cc = conv_w.shape[1]
    const = lambda shape: pl.BlockSpec(shape, lambda i, c: (0,) * len(shape))
    r = min(SSD_STEP_ROWS, s)
    return pl.pallas_call(
        _ssd_kernel,
        grid=(b, s // r),
        in_specs=[
            pl.BlockSpec((1, r, cc), lambda i, c: (i, c, COL_XBC // cc)),
            pl.BlockSpec((1, r, inner), lambda i, c: (i, c, COL_Z // inner)),
            pl.BlockSpec((1, r, LANES), lambda i, c: (i, c, 0)),
            const((SSD_CONV, cc)), const((1, cc)), const((1, LANES)), const((1, LANES)),
            const((1, inner)), const((1, inner)), const((LANES, inner)),
        ],
        out_specs=[
            pl.BlockSpec((1, r, inner), lambda i, c: (i, c, 0)),
            pl.BlockSpec((1, LANES, r), lambda i, c: (i, 0, c)),
        ],
        out_shape=[
            jax.ShapeDtypeStruct((b, s, inner), BF16),
            jax.ShapeDtypeStruct((b, LANES, s), F32),
        ],
        scratch_shapes=[
            pltpu.VMEM((SSD_STATE, inner), F32),
            pltpu.VMEM((L + CONV_HALO, cc), F32),
            pltpu.VMEM((1, LANES), F32),
        ],
        compiler_params=_params(("arbitrary", "arbitrary")),
        name="ssd",
    )(main3, main3, small3, conv_w, conv_b, bias_row, alog_row, dskip_row, normg_row, e_mat)


def _fox_kernel(q_ref, k_ref, v_ref, ck_ref, o_ref, m_sc, l_sc, acc_sc, *, tq, tk):
    h = pl.program_id(1)
    i = pl.program_id(2)
    m_sc[...] = jnp.full(m_sc.shape, NEG_BIG, F32)
    l_sc[...] = jnp.zeros_like(l_sc)
    acc_sc[...] = jnp.zeros_like(acc_sc)
    reps = tk // LANES

    def step(j, masked, r0):
        rows = slice(r0, tq)
        ks = pl.multiple_of(j * tk, tk)
        k = k_ref[0, pl.ds(ks, tk), :]
        v = v_ref[0, pl.ds(ks, tk), :]
        s = _dot_nt(q_ref[0, rows, :], k) - ck_ref[0, pl.ds(h, 1), pl.ds(ks, tk)]
        if masked:
            row = i * tq + r0 + lax.broadcasted_iota(jnp.int32, (tq - r0, tk), 0)
            col = j * tk + lax.broadcasted_iota(jnp.int32, (tq - r0, tk), 1)
            s = jnp.where(row >= col, s, -jnp.inf)
        m_prev = m_sc[rows, :]
        m_new = jnp.maximum(m_prev, jnp.max(s, axis=-1, keepdims=True))
        alpha = jnp.exp2(m_prev - m_new)
        p = jnp.exp2(s - jnp.concatenate([m_new] * reps, axis=1))
        l_sc[rows, :] = alpha * l_sc[rows, :] + jnp.sum(p, axis=-1, keepdims=True)
        acc_sc[rows, :] = alpha * acc_sc[rows, :] + _dot(p.astype(BF16), v)
        m_sc[rows, :] = m_new

    def body(j, carry):
        step(j, False, 0)
        return carry

    if tk >= tq:
        n_full, n_diag = (i * tq) // tk, 1
    else:
        n_full, n_diag = i * (tq // tk), tq // tk
    lax.fori_loop(0, n_full, body, 0)
    for jd in range(n_diag):
        step(n_full + jd, True, jd * tk if tk < tq else 0)
    o_ref[0] = (acc_sc[...] / l_sc[...]).astype(BF16)


def _fox(main3, ct3, tq, tk):
    b, s, _ = main3.shape
    dh = FOX_HEAD_DIM
    assert dh == LANES and (tk % tq == 0 or tq % tk == 0)
    return pl.pallas_call(
        functools.partial(_fox_kernel, tq=tq, tk=tk),
        grid=(b, FOX_HEADS, s // tq),
        in_specs=[
            pl.BlockSpec((1, tq, dh), lambda bi, h, i: (bi, i, COL_Q // dh + h)),
            pl.BlockSpec((1, s, dh), lambda bi, h, i: (bi, 0, COL_K // dh + h)),
            pl.BlockSpec((1, s, dh), lambda bi, h, i: (bi, 0, COL_V // dh + h)),
            pl.BlockSpec((1, FOX_HEADS, s), lambda bi, h, i: (bi, LANE_F // FOX_HEADS, 0)),
        ],
        out_specs=pl.BlockSpec((1, tq, dh), lambda bi, h, i: (bi, i, h)),
        out_shape=jax.ShapeDtypeStruct((b, s, FOX_HEADS * dh), BF16),
        scratch_shapes=[pltpu.VMEM((tq, LANES), F32), pltpu.VMEM((tq, LANES), F32), pltpu.VMEM((tq, dh), F32)],
        compiler_params=_params(("arbitrary", "arbitrary", "arbitrary")),
        name="fox",
    )(main3, main3, main3, ct3)


def _memkv_kernel(mem_ref, g_ref, w_ref, kv_ref):
    kv_ref[0] = _dot(_rms(mem_ref[0], g_ref[...]).astype(BF16), w_ref[...]).astype(BF16)


def _memkv(mem, g, w_kv):
    b, m, d = mem.shape
    return pl.pallas_call(
        _memkv_kernel,
        grid=(b,),
        in_specs=[
            pl.BlockSpec((1, m, d), lambda i: (i, 0, 0)),
            pl.BlockSpec((1, d), lambda i: (0, 0)),
            pl.BlockSpec((d, 2 * d), lambda i: (0, 0)),
        ],
        out_specs=pl.BlockSpec((1, m, 2 * d), lambda i: (i, 0, 0)),
        out_shape=jax.ShapeDtypeStruct((b, m, 2 * d), BF16),
        compiler_params=_params(("arbitrary",)),
        name="memkv",
    )(mem, g, w_kv)


def _cross_kernel(ys_ref, yf_ref, gt_ref, bg_ref, x_ref, ws_ref, wf_ref, wm_ref,
                  kv_ref, gc_ref, wq_ref, wo_ref, gf_ref, wr_ref, br_ref,
                  h2_ref, u3_ref, route_ref, cnt_ref, base):
    tm, d = x_ref.shape
    dh = d // X_HEADS

    @pl.when(pl.program_id(0) == 0)
    def _():
        base[...] = jnp.zeros_like(base)

    g = _sigmoid(gt_ref[...].astype(F32) + bg_ref[...])
    merged = g[:, :d] * _dot(ys_ref[...], ws_ref[...]) + g[:, d:] * _dot(yf_ref[...], wf_ref[...])
    h1 = x_ref[...] + _dot(merged.astype(BF16), wm_ref[...])

    q = _dot(_rms(h1, gc_ref[...]).astype(BF16), wq_ref[...]).astype(BF16)
    outs = []
    for hd in range(X_HEADS):
        kh = kv_ref[0, :, hd * dh:(hd + 1) * dh]
        vh = kv_ref[0, :, d + hd * dh:d + (hd + 1) * dh]
        s = _dot_nt(q[:, hd * dh:(hd + 1) * dh], kh) * (dh ** -0.5)
        p = jnp.exp(s - jnp.max(s, axis=-1, keepdims=True))
        outs.append(_dot(p.astype(BF16), vh) / jnp.sum(p, axis=-1, keepdims=True))
    h2 = h1 + _dot(jnp.concatenate(outs, axis=1).astype(BF16), wo_ref[...])
    h2_ref[...] = h2
    u3 = _rms(h2, gf_ref[...])
    _store_rows(u3_ref, u3)

    u_hi = u3.astype(BF16)
    u_lo = (u3 - u_hi.astype(F32)).astype(BF16)
    w_r = wr_ref[...]
    w_hi = w_r.astype(BF16)
    w_lo = (w_r - w_hi.astype(F32)).astype(BF16)
    logits = _dot(u_hi, w_hi) + _dot(u_hi, w_lo) + _dot(u_lo, w_hi) + br_ref[...]
    lane = lax.broadcasted_iota(jnp.int32, (tm, LANES), 1)
    lane_f = lane.astype(F32)
    vals, idxs = [], []
    work = logits
    for _ in range(TOP_K):
        m = jnp.max(work, axis=-1, keepdims=True)
        ix = jnp.min(jnp.where(work == m, lane_f, float(LANES)), axis=-1, keepdims=True)
        vals.append(m)
        idxs.append(ix)
        work = jnp.where(lane_f == ix, -jnp.inf, work)
    exps = [jnp.exp(v - vals[0]) for v in vals]
    den = exps[0] + exps[1] + exps[2] + exps[3]

    onehot = jnp.zeros((tm, LANES), F32)
    for ix in idxs:
        onehot = onehot + jnp.where(lane_f == ix, 1.0, 0.0)
    row = lax.broadcasted_iota(jnp.int32, (tm, tm), 0)
    col = lax.broadcasted_iota(jnp.int32, (tm, tm), 1)
    strict = jnp.where(row > col, 1.0, 0.0).astype(BF16)
    rank = _dot(strict, onehot.astype(BF16)) + base[...]
    route = jnp.zeros((tm, LANES), F32)
    for r in range(TOP_K):
        pos = jnp.sum(jnp.where(lane_f == idxs[r], rank, 0.0), axis=-1, keepdims=True)
        route = jnp.where(lane == LANE_IDX + r, idxs[r], route)
        route = jnp.where(lane == LANE_POS + r, pos, route)
        route = jnp.where(lane == LANE_GATE + r, exps[r] / den, route)
    route_ref[...] = route
    new_base = base[...] + jnp.sum(onehot, axis=0, keepdims=True)
    base[...] = new_base
    cnt_ref[...] = new_base


def _cross(y_ssd, y_fox, main2, b_gates, x2, w_s, w_f, w_m, kv, g_cross, w_q, w_o, g_ffn, w_r, b_r, tm, seq):
    n, d = x2.shape
    m = kv.shape[1]
    per_batch = seq // tm
    const = lambda shape: pl.BlockSpec(shape, lambda i: (0,) * len(shape))
    return pl.pallas_call(
        _cross_kernel,
        grid=(n // tm,),
        in_specs=[
            pl.BlockSpec((tm, d), lambda i: (i, 0)),
            pl.BlockSpec((tm, d), lambda i: (i, 0)),
            pl.BlockSpec((tm, 2 * d), lambda i: (i, COL_GATE // (2 * d))),
            const((1, 2 * d)),
            pl.BlockSpec((tm, d), lambda i: (i, 0)),
            const((d, d)), const((d, d)), const((d, d)),
            pl.BlockSpec((1, m, 2 * d), lambda i: (i // per_batch, 0, 0)),
            const((1, d)), const((d, d)), const((d, d)), const((1, d)),
            const((d, LANES)), const((1, LANES)),
        ],
        out_specs=[
            pl.BlockSpec((tm, d), lambda i: (i, 0)),
            pl.BlockSpec((tm * ROW_SUB, LANES), lambda i: (i, 0)),
            pl.BlockSpec((tm, LANES), lambda i: (i, 0)),
            pl.BlockSpec((1, LANES), lambda i: (0, 0)),
        ],
        out_shape=[
            jax.ShapeDtypeStruct((n, d), F32),
            jax.ShapeDtypeStruct((n * ROW_SUB, LANES), F32),
            jax.ShapeDtypeStruct((n, LANES), F32),
            jax.ShapeDtypeStruct((1, LANES), F32),
        ],
        scratch_shapes=[pltpu.VMEM((1, LANES), F32)],
        compiler_params=_params(("arbitrary",)),
        name="cross_router",
    )(y_ssd, y_fox, main2, b_gates, x2, w_s, w_f, w_m, kv, g_cross, w_q, w_o, g_ffn, w_r, b_r)


def _dispatch_kernel(zb_ref, dest_ref, u_ref, xr_ref, zbuf, sem, zsem, *, td):
    i = pl.program_id(0)

    @pl.when(i == 0)
    def _():
        zbuf[...] = jnp.zeros_like(zbuf)

        def zero_copy(e):
            blk = EXPERT_ROWS * ROW_SUB
            return pltpu.make_async_copy(zbuf, xr_ref.at[pl.ds(pl.multiple_of(zb_ref[e] * blk, blk), blk)], zsem)

        for e in range(2 * N_EXPERTS):
            @pl.when(zb_ref[e] >= 0)
            def _():
                zero_copy(e).start()
        for e in range(2 * N_EXPERTS):
            @pl.when(zb_ref[e] >= 0)
            def _():
                zero_copy(e).wait()

    def issue(t, carry):
        for k in range(TOP_K):
            pltpu.make_async_copy(_row_tile(u_ref, t), _row_tile(xr_ref, dest_ref[t * TOP_K + k]),
                                  sem).start(priority=k % 2)
        return carry

    lax.fori_loop(0, td, issue, 0, unroll=4)
    for k in range(TOP_K):
        pltpu.make_async_copy(u_ref, xr_ref.at[pl.ds(0, td * ROW_SUB)], sem).wait()


def _row_tile(ref, r, lead=()):
    return ref.at[lead + (pl.ds(pl.multiple_of(r * ROW_SUB, ROW_SUB), ROW_SUB),)]


def _dispatch(zero_blk, dest_flat, u3, n_rows, td):
    n = u3.shape[0] // ROW_SUB
    return pl.pallas_call(
        functools.partial(_dispatch_kernel, td=td),
        grid_spec=pltpu.PrefetchScalarGridSpec(
            num_scalar_prefetch=1,
            grid=(n // td,),
            in_specs=[
                pl.BlockSpec((td * TOP_K,), lambda i, zb: (i,), memory_space=pltpu.SMEM),
                pl.BlockSpec((td * ROW_SUB, LANES), lambda i, zb: (i, 0)),
            ],
            out_specs=pl.BlockSpec(memory_space=pl.ANY),
            scratch_shapes=[
                pltpu.VMEM((EXPERT_ROWS * ROW_SUB, LANES), F32),
                pltpu.SemaphoreType.DMA(()),
                pltpu.SemaphoreType.DMA(()),
            ],
        ),
        out_shape=jax.ShapeDtypeStruct((n_rows * ROW_SUB, LANES), F32),
        compiler_params=_params(("arbitrary",)),
        name="dispatch",
    )(zero_blk, dest_flat, u3)


def _expert_kernel(be_ref, nu_ref, x_ref, w1_ref, b1_ref, w2_ref, b2_ref, y_ref, w1b, w2b):
    f = w2_ref.shape[1]
    i = pl.program_id(0)
    used = i < nu_ref[0]

    @pl.when(jnp.logical_and(used, jnp.logical_or(i == 0, be_ref[i] != be_ref[jnp.maximum(i - 1, 0)])))
    def _():
        w1b[...] = w1_ref[0].astype(BF16)
        w2b[...] = w2_ref[0].astype(BF16)

    @pl.when(used)
    def _():
        hgl = _dot(_load_rows(x_ref, EXPERT_ROWS).astype(BF16), w1b[...]) + b1_ref[0]
        glu = jnp.minimum(hgl[:, :f], SWIGLU_LIMIT)
        lin = jnp.clip(hgl[:, f:], -SWIGLU_LIMIT, SWIGLU_LIMIT)
        act = glu * _sigmoid(SWIGLU_ALPHA * glu) * (lin + 1.0)
        _store_rows(y_ref, _dot(act.astype(BF16), w2b[...]) + b2_ref[0])

    @pl.when(jnp.logical_not(used))
    def _():
        y_ref[...] = jnp.zeros_like(y_ref)


def _experts(blk_e, n_used, x_rows, w1, b1, w2, b2):
    f, d = w2.shape[1], w2.shape[2]
    blk = EXPERT_ROWS * ROW_SUB
    nb = x_rows.shape[0] // blk
    row_blk = lambda i, be, nu: (jnp.minimum(i, nu[0] - 1), 0)
    return pl.pallas_call(
        _expert_kernel,
        grid_spec=pltpu.PrefetchScalarGridSpec(
            num_scalar_prefetch=2,
            grid=(nb,),
            in_specs=[
                pl.BlockSpec((blk, LANES), row_blk),
                pl.BlockSpec((1, d, 2 * f), lambda i, be, nu: (be[i], 0, 0)),
                pl.BlockSpec((1, 1, 2 * f), lambda i, be, nu: (be[i], 0, 0)),
                pl.BlockSpec((1, f, d), lambda i, be, nu: (be[i], 0, 0)),
                pl.BlockSpec((1, 1, d), lambda i, be, nu: (be[i], 0, 0)),
            ],
            out_specs=pl.BlockSpec((blk, LANES), lambda i, be, nu: (i, 0)),
            scratch_shapes=[pltpu.VMEM((d, 2 * f), BF16), pltpu.VMEM((f, d), BF16)],
        ),
        out_shape=jax.ShapeDtypeStruct(x_rows.shape, F32),
        compiler_params=_params(("arbitrary",), EXPERT_VMEM_LIMIT),
        name="experts",
    )(blk_e, n_used, x_rows, w1, b1, w2, b2)


def _combine_kernel(dest_ref, y_ref, route_ref, h_ref, g_ref, o_ref, ybuf, sem, *, tc):
    def issue(t, carry):
        for k in range(TOP_K):
            pltpu.make_async_copy(_row_tile(y_ref, dest_ref[t * TOP_K + k]), _row_tile(ybuf, t, (k,)),
                                  sem).start(priority=k % 2)
        return carry

    lax.fori_loop(0, tc, issue, 0, unroll=4)
    for k in range(TOP_K):
        pltpu.make_async_copy(y_ref.at[pl.ds(0, tc * ROW_SUB)], ybuf.at[k], sem).wait()

    route = route_ref[...]
    lane = lax.broadcasted_iota(jnp.int32, route.shape, 1)
    acc = h_ref[...]
    for k in range(TOP_K):
        gate = jnp.sum(jnp.where(lane == LANE_GATE + k, route, 0.0), axis=-1, keepdims=True)
        acc = acc + gate * _load_rows(ybuf, tc, (k,))
    o_ref[...] = _rms(acc, g_ref[...])


def _combine(dest_flat, y_rows, route, h2, g_final, tc):
    n, d = h2.shape
    return pl.pallas_call(
        functools.partial(_combine_kernel, tc=tc),
        grid=(n // tc,),
        in_specs=[
            pl.BlockSpec((tc * TOP_K,), lambda i: (i,), memory_space=pltpu.SMEM),
            pl.BlockSpec(memory_space=pl.ANY),
            pl.BlockSpec((tc, LANES), lambda i: (i, 0)),
            pl.BlockSpec((tc, d), lambda i: (i, 0)),
            pl.BlockSpec((1, d), lambda i: (0, 0)),
        ],
        out_specs=pl.BlockSpec((tc, d), lambda i: (i, 0)),
        out_shape=jax.ShapeDtypeStruct((n, d), F32),
        scratch_shapes=[pltpu.VMEM((TOP_K, tc * ROW_SUB, LANES), F32), pltpu.SemaphoreType.DMA(())],
        compiler_params=_params(("arbitrary",)),
        name="combine",
    )(dest_flat, y_rows, route, h2, g_final)


def _pad_lanes(v, offset=0, fill=0.0):
    out = jnp.full((1, LANES), fill, F32)
    return lax.dynamic_update_slice(out, v.reshape(1, -1).astype(F32), (0, offset))


def _layer(h, mem, norm_mix_g, w_in, conv_w, conv_b, dt_bias, a_log, d_skip, ssd_norm_g,
           w_ssd_branch, b_forget, w_fox_branch, b_gates, w_out, norm_cross_g, norm_mem_g,
           w_cross_q, w_cross_kv, w_cross_o, norm_ffn_g, w_router, b_router,
           w_expert_in, b_expert_in, w_expert_out, b_expert_out, final_g):
    b, s, d = h.shape
    n = b * s
    inner = SSD_HEADS * SSD_HEAD_DIM
    cc = inner + 2 * SSD_GROUPS * SSD_STATE
    tm = min(512, s)

    cuts = [inner, cc, SSD_HEADS, d, d, d, FOX_HEADS, 2 * d]
    offs = [0]
    for c in cuts:
        offs.append(offs[-1] + c)
    wz, wxbc, wdt, wq, wk, wv, wf, wg = [w_in[:, offs[i]:offs[i + 1]] for i in range(8)]
    w_main = jnp.concatenate([wxbc, wz, wq * (LOG2E * FOX_HEAD_DIM ** -0.5), wk, wv, wg], axis=1).astype(BF16)
    w_small = jnp.concatenate(
        [wdt, wf, jnp.zeros((d, LANES - SSD_HEADS - FOX_HEADS), F32)], axis=1).astype(BF16)

    x2 = h.reshape(n, d)
    main, small = _inproj(x2, norm_mix_g.reshape(1, d), w_main, w_small, min(1024, n), 2048)
    main3 = main.reshape(b, s, MAIN_COLS)

    bias_row = _pad_lanes(jnp.concatenate([dt_bias, b_forget]))
    alog_row = _pad_lanes(a_log)
    dskip_row = jnp.repeat(d_skip, SSD_HEAD_DIM).reshape(1, inner)
    e_mat = (jnp.arange(LANES)[:, None] == (jnp.arange(inner)[None, :] // SSD_HEAD_DIM)).astype(BF16)
    y_ssd, ct3 = _ssd(main3, small.reshape(b, s, LANES), conv_w, conv_b.reshape(1, cc), bias_row,
                      alog_row, dskip_row, ssd_norm_g.reshape(1, inner), e_mat)
    y_fox = _fox(main3, ct3, min(FOX_TQ, s), min(FOX_TK, s))

    kv = _memkv(mem, norm_mem_g.reshape(1, d), w_cross_kv.astype(BF16))
    w_r = jnp.concatenate([w_router, jnp.zeros((d, LANES - N_EXPERTS), F32)], axis=1)
    b_r = _pad_lanes(b_router, fill=NEG_BIG)
    h2, u3, route, counts = _cross(
        y_ssd.reshape(n, inner), y_fox.reshape(n, d), main, b_gates.reshape(1, 2 * d), x2,
        w_ssd_branch.astype(BF16), w_fox_branch.astype(BF16), w_out.astype(BF16),
        kv, norm_cross_g.reshape(1, d), w_cross_q.astype(BF16), w_cross_o.astype(BF16),
        norm_ffn_g.reshape(1, d), w_r, b_r, tm, s)

    idx = route[:, LANE_IDX:LANE_IDX + TOP_K].astype(jnp.int32)
    pos = route[:, LANE_POS:LANE_POS + TOP_K].astype(jnp.int32)
    cnt = counts[0, :N_EXPERTS].astype(jnp.int32)
    nblk = (cnt + EXPERT_ROWS - 1) // EXPERT_ROWS
    bend = jnp.cumsum(nblk)
    bstart = bend - nblk
    n_used = bend[-1:]
    dest = (bstart[idx] * EXPERT_ROWS + pos).reshape(n * TOP_K)
    nb = n * TOP_K // EXPERT_ROWS + N_EXPERTS
    blk = jnp.minimum(jnp.arange(nb, dtype=jnp.int32), n_used[0] - 1)
    blk_e = jnp.minimum(jnp.sum(blk[:, None] >= bend[None, :], axis=1), N_EXPERTS - 1).astype(jnp.int32)
    tail = n_used[0] + jnp.arange(N_EXPERTS, dtype=jnp.int32)
    zero_blk = jnp.concatenate([jnp.where(nblk > 0, bend - 1, -1),
                                jnp.where(tail < nb, tail, -1)]).astype(jnp.int32)

    x_rows = _dispatch(zero_blk, dest, u3, nb * EXPERT_ROWS, min(DISPATCH_TOKENS, n))
    y_rows = _experts(blk_e, n_used.astype(jnp.int32), x_rows, w_expert_in,
                      b_expert_in.reshape(N_EXPERTS, 1, -1), w_expert_out,
                      b_expert_out.reshape(N_EXPERTS, 1, -1))
    out = _combine(dest, y_rows, route, h2, final_g.reshape(1, d), min(COMBINE_TOKENS, n))
    return out.reshape(b, s, d)


def kernel(x, mem, norm_mix_g, w_in, conv_w, conv_b, dt_bias, a_log, d_skip, ssd_norm_g, w_ssd_branch,
           b_forget, w_fox_branch, b_gates, w_out, norm_cross_g, norm_mem_g, w_cross_q, w_cross_kv,
           w_cross_o, norm_ffn_g, w_router, b_router, w_expert_in, b_expert_in, w_expert_out,
           b_expert_out, norm_final_g):
    assert x.shape[2] == FOX_HEADS * FOX_HEAD_DIM == SSD_HEADS * SSD_HEAD_DIM
    assert norm_mix_g.shape[0] == 1, "single-layer problem"
    l = 0
    return _layer(x, mem, norm_mix_g[l], w_in[l], conv_w[l], conv_b[l], dt_bias[l], a_log[l], d_skip[l],
                  ssd_norm_g[l], w_ssd_branch[l], b_forget[l], w_fox_branch[l], b_gates[l], w_out[l],
                  norm_cross_g[l], norm_mem_g[l], w_cross_q[l], w_cross_kv[l], w_cross_o[l],
                  norm_ffn_g[l], w_router[l], b_router[l], w_expert_in[l], b_expert_in[l],
                  w_expert_out[l], b_expert_out[l], norm_final_g)
```
